```python
import math
import jax
import jax.numpy as jnp
from jax import lax
import numpy as np

D_MODEL = 1024
BATCH = 4
SEQ = 4096
DEPTH = 4
DEC_BATCH = 128
DEC_SEQ = 8
PAST_LEN = 2048
PAGE_SIZE = 128

N_MIXERS = 3
D_INNER = 2 * D_MODEL
SSM_GROUP = 16
SSM_GROUPS = D_INNER // SSM_GROUP
SSM_STATE = 64
SSM_CHUNK = 128
DT_MIN = 1e-3
DT_MAX = 1e-1
DIFF_HEAD_DIM = 64
DIFF_HEADS = D_INNER // (2 * DIFF_HEAD_DIM)
DIFF_ROW = 2 * DIFF_HEAD_DIM
ROPE_DIM = DIFF_HEAD_DIM // 4
ROPE_THETA = 500000.0
Q_BLOCK = 128
HGRN_HEADS = 16
HGRN_DK = D_INNER // HGRN_HEADS
HGRN_DV = D_INNER // HGRN_HEADS
HGRN_CHUNK = 64
LN_EPS = 1e-5
RMS_EPS = 1e-6
DEEPNORM_ALPHA = (2 * DEPTH) ** 0.25
DEEPNORM_BETA = (8 * DEPTH) ** -0.25
SSM_LAYERS = tuple(i for i in range(DEPTH) if i % N_MIXERS == 0)
ATTN_LAYERS = tuple(i for i in range(DEPTH) if i % N_MIXERS == 1)
HGRN_LAYERS = tuple(i for i in range(DEPTH) if i % N_MIXERS == 2)
N_SSM = len(SSM_LAYERS)
N_ATTN = len(ATTN_LAYERS)
N_HGRN = len(HGRN_LAYERS)

kernel_name = "hybrid_s5_diffattn_hgrn2_step"

F32 = jnp.float32


def layer_norm(x, g, b):
    xf = x.astype(F32)
    mu = jnp.mean(xf, -1, keepdims=True)
    var = jnp.mean(jnp.square(xf - mu), -1, keepdims=True)
    return ((xf - mu) * lax.rsqrt(var + LN_EPS)).astype(x.dtype) * g + b


def rms_norm(x, g):
    xf = x.astype(F32)
    return (xf * lax.rsqrt(jnp.mean(xf * xf, -1, keepdims=True) + RMS_EPS)).astype(x.dtype) * g


def rope_partial(x, pos):
    half = ROPE_DIM // 2
    inv = ROPE_THETA ** (-jnp.arange(half, dtype=F32) / half)
    ang = pos.astype(F32)[:, None] * inv[None, :]
    cos = jnp.cos(ang)[:, None, :]
    sin = jnp.sin(ang)[:, None, :]
    xr = x[..., :ROPE_DIM].astype(F32)
    x1, x2 = xr[..., :half], xr[..., half:]
    rot = jnp.concatenate([x1 * cos - x2 * sin, x2 * cos + x1 * sin], -1).astype(x.dtype)
    return jnp.concatenate([rot, x[..., ROPE_DIM:]], -1)


def s5_discretize(a_re, a_im, b_re, b_im, log_dt):
    dt = jnp.exp(log_dt.astype(F32))[:, None]
    lr, li = a_re.astype(F32), a_im.astype(F32)
    mag = jnp.exp(lr * dt)
    lb_re, lb_im = mag * jnp.cos(li * dt), mag * jnp.sin(li * dt)
    nr, ni = lb_re - 1.0, lb_im
    den = lr * lr + li * li
    w_re = (nr * lr + ni * li) / den
    w_im = (ni * lr - nr * li) / den
    br, bi = b_re.astype(F32), b_im.astype(F32)
    bb_re = w_re[..., None] * br - w_im[..., None] * bi
    bb_im = w_re[..., None] * bi + w_im[..., None] * br
    return lb_re, lb_im, bb_re, bb_im


def _complex_affine_combine(e1, e2):
    a1r, a1i, b1r, b1i = e1
    a2r, a2i, b2r, b2i = e2
    return (a1r * a2r - a1i * a2i, a1r * a2i + a1i * a2r,
            a2r * b1r - a2i * b1i + b2r, a2r * b1i + a2i * b1r + b2i)


def s5_scan(u, h0_re, h0_im, lb_re, lb_im, bb_re, bb_im, c_re, c_im):
    n, L = u.shape[:2]
    T = math.gcd(SSM_CHUNK, L)
    nc = L // T
    uc = jnp.moveaxis(u.reshape(n, nc, T, SSM_GROUPS, SSM_GROUP), 1, 0)

    def block_step(carry, u_t):
        hr, hi = carry
        bu_re = jnp.einsum('ntgc,gpc->ntgp', u_t, bb_re)
        bu_im = jnp.einsum('ntgc,gpc->ntgp', u_t, bb_im)
        bu_re = bu_re.at[:, 0].add(lb_re * hr - lb_im * hi)
        bu_im = bu_im.at[:, 0].add(lb_re * hi + lb_im * hr)
        ar = jnp.broadcast_to(lb_re, bu_re.shape)
        ai = jnp.broadcast_to(lb_im, bu_im.shape)
        _, _, sr, si = lax.associative_scan(_complex_affine_combine, (ar, ai, bu_re, bu_im), axis=1)
        y = jnp.einsum('ntgp,gcp->ntgc', sr, c_re) - jnp.einsum('ntgp,gcp->ntgc', si, c_im)
        return (sr[:, -1], si[:, -1]), y

    (hr, hi), y = lax.scan(block_step, (h0_re, h0_im), uc)
    return jnp.moveaxis(y, 0, 1).reshape(n, L, SSM_GROUPS, SSM_GROUP), hr, hi


def s5_branch(h, h0_re, h0_im, w_in, a_re, a_im, b_re, b_im, c_re, c_im, d_skip, log_dt, w_glu, b_glu, w_out):
    n, L, _ = h.shape
    uz = h @ w_in
    u, z = uz[..., :D_INNER], uz[..., D_INNER:]
    lb_re, lb_im, bb_re, bb_im = s5_discretize(a_re, a_im, b_re, b_im, log_dt)
    y, hr, hi = s5_scan(u.astype(F32).reshape(n, L, SSM_GROUPS, SSM_GROUP),
                        h0_re.astype(F32), h0_im.astype(F32), lb_re, lb_im, bb_re, bb_im,
                        c_re.astype(F32), c_im.astype(F32))
    y = y.reshape(n, L, D_INNER).astype(h.dtype) + d_skip * u
    y = jax.nn.gelu(y)
    y = y * jax.nn.sigmoid(y @ w_glu + b_glu)
    return (y * jax.nn.silu(z)) @ w_out, hr.astype(h.dtype), hi.astype(h.dtype)


def diff_attn_project(h, pos, w_in):
    n, L, _ = h.shape
    q, k, v, z = jnp.split(h @ w_in, 4, axis=-1)
    q = rope_partial(q.reshape(n, L, 2 * DIFF_HEADS, DIFF_HEAD_DIM), pos).reshape(n, L, DIFF_HEADS, DIFF_ROW)
    k = rope_partial(k.reshape(n, L, 2 * DIFF_HEADS, DIFF_HEAD_DIM), pos).reshape(n, L, DIFF_HEADS, DIFF_ROW)
    v = v.reshape(n, L, DIFF_HEADS, DIFF_ROW)
    return q, k, v, z


def diff_scores(q, k):
    n, tq = q.shape[:2]
    tk = k.shape[1]
    qs = q.reshape(n, tq, DIFF_HEADS, 2, DIFF_HEAD_DIM)
    ks = k.reshape(n, tk, DIFF_HEADS, 2, DIFF_HEAD_DIM)
    return jnp.einsum('nqhjd,nkhjd->njhqk', qs, ks).astype(F32) * (DIFF_HEAD_DIM ** -0.5)


def diff_weights(s, mask, lam):
    p = jax.nn.softmax(jnp.where(mask, s, -jnp.inf), axis=-1)
    return p[:, 0] - lam * p[:, 1]


def diff_attn_prompt(q, k, v, lam):
    n, L = q.shape[:2]
    blk = math.gcd(Q_BLOCK, L)
    nb = L // blk
    qb = jnp.moveaxis(q.reshape(n, nb, blk, DIFF_HEADS, DIFF_ROW), 1, 0)
    kpos = jnp.arange(L)

    def one_block(args):
        qi, b = args
        qpos = b * blk + jnp.arange(blk)
        w = diff_weights(diff_scores(qi, k), qpos[:, None] >= kpos[None, :], lam)
        return jnp.einsum('nhqk,nkhe->nqhe', w.astype(v.dtype), v)

    out = lax.map(one_block, (qb, jnp.arange(nb)))
    return jnp.moveaxis(out, 0, 1).reshape(n, L, DIFF_HEADS, DIFF_ROW)


def diff_attn_cached(q, k, v, k_past, v_past, lam):
    T = q.shape[1]
    P = k_past.shape[1]
    s = jnp.concatenate([diff_scores(q, k_past), diff_scores(q, k)], axis=-1)
    mask = jnp.concatenate([jnp.ones((T, P), bool), jnp.tril(jnp.ones((T, T), bool))], axis=-1)
    w = diff_weights(s, mask, lam).astype(v.dtype)
    return (jnp.einsum('nhqk,nkhe->nqhe', w[..., :P], v_past)
            + jnp.einsum('nhqk,nkhe->nqhe', w[..., P:], v))


def hgrn2_scan(q, log_f, k, v, s0):
    n, L = q.shape[:2]
    T = math.gcd(HGRN_CHUNK, L)
    nc = L // T

    def to_blocks(a):
        return jnp.moveaxis(a.reshape(n, nc, T, *a.shape[2:]), 1, 0)

    tri = jnp.tril(jnp.ones((T, T), bool))[None, :, :, None, None]

    def block_step(S, inp):
        qc, lfc, kc, vc = inp
        b = jnp.cumsum(lfc, axis=1)
        o_inter = jnp.einsum('nthk,nhkv->nthv', qc * jnp.exp(b), S)
        decay = jnp.exp(jnp.where(tri, b[:, :, None] - b[:, None, :], -jnp.inf))
        att = jnp.einsum('nthk,ntshk,nshk->nhts', qc, decay, kc)
        o_intra = jnp.einsum('nhts,nshv->nthv', att, vc)
        bT = b[:, -1]
        S_new = jnp.exp(bT)[..., None] * S + jnp.einsum('nshk,nshv->nhkv', kc * jnp.exp(bT[:, None] - b), vc)
        return S_new, o_inter + o_intra

    S, o = lax.scan(block_step, s0, (to_blocks(q), to_blocks(log_f), to_blocks(k), to_blocks(v)))
    return jnp.moveaxis(o, 0, 1).reshape(n, L, HGRN_HEADS, HGRN_DV), S


def hgrn2_branch(h, s0, w_in, lb, g_norm, w_out):
    n, L, _ = h.shape
    q, fz, vi, z = jnp.split(h @ w_in, 4, axis=-1)
    q = q.reshape(n, L, HGRN_HEADS, HGRN_DK).astype(F32)
    fz = fz.reshape(n, L, HGRN_HEADS, HGRN_DK).astype(F32)
    vi = vi.reshape(n, L, HGRN_HEADS, HGRN_DV).astype(F32)
    lbh = lb.astype(F32).reshape(HGRN_HEADS, HGRN_DK)
    log_f = jnp.logaddexp(jnp.log(lbh), jnp.log1p(-lbh) + jax.nn.log_sigmoid(fz))
    k = (1.0 - lbh) * jax.nn.sigmoid(-fz)
    o, S = hgrn2_scan(q, log_f, k, vi, s0.astype(F32))
    o = rms_norm(o, g_norm.astype(F32).reshape(HGRN_HEADS, HGRN_DV)).reshape(n, L, D_INNER).astype(h.dtype)
    return (o * jax.nn.silu(z)) @ w_out, S.astype(h.dtype)


def setup_inputs(seed: int = 0) -> dict:
    key = jax.random.key(seed)
    keys = jax.random.split(key, 64)
    cnt = [0]

    def nk():
        cnt[0] += 1
        return keys[cnt[0] - 1]

    def nrm(shape, s):
        return jax.random.normal(nk(), shape, F32) * s

    n_pages = PAST_LEN // PAGE_SIZE
    n_used = DEC_BATCH * n_pages
    n_pool = n_used + max(1, n_used // 4)
    page_table = jax.random.permutation(nk(), n_pool)[:n_used].reshape(DEC_BATCH, n_pages).astype(jnp.int32)
    d_in = D_MODEL ** -0.5
    e_in = D_INNER ** -0.5
    return {
        "x_prompt": nrm((BATCH, SEQ, D_MODEL), 1.0),
        "x_sample": nrm((DEC_BATCH, DEC_SEQ, D_MODEL), 1.0),
        "cache_attn_k": nrm((n_pool, N_ATTN, PAGE_SIZE, DIFF_HEADS, DIFF_ROW), 1.0),
        "cache_attn_v": nrm((n_pool, N_ATTN, PAGE_SIZE, DIFF_HEADS, DIFF_ROW), 1.0),
        "state_ssm_re": nrm((DEC_BATCH, N_SSM, SSM_GROUPS, SSM_STATE), 0.5),
        "state_ssm_im": nrm((DEC_BATCH, N_SSM, SSM_GROUPS, SSM_STATE), 0.5),
        "state_hgrn": nrm((DEC_BATCH, N_HGRN, HGRN_HEADS, HGRN_DK, HGRN_DV), 0.5),
        "page_table": page_table,
        "c_prompt": nrm((BATCH, D_MODEL), 1.0),
        "c_sample": nrm((DEC_BATCH, D_MODEL), 1.0),
        "ada_w": nrm((DEPTH, D_MODEL, 3 * D_MODEL), 0.02),
        "ada_b": nrm((DEPTH, 3 * D_MODEL), 0.02),
        "ln_g": 1.0 + nrm((DEPTH, D_MODEL), 0.02),
        "ln_b": nrm((DEPTH, D_MODEL), 0.02),
        "ssm_w_in": nrm((N_SSM, D_MODEL, 2 * D_INNER), d_in),
        "ssm_a_re": -0.5 + nrm((N_SSM, SSM_GROUPS, SSM_STATE), 0.01),
        "ssm_a_im": jnp.pi * jnp.arange(SSM_STATE, dtype=F32) + nrm((N_SSM, SSM_GROUPS, SSM_STATE), 0.01),
        "ssm_b_re": nrm((N_SSM, SSM_GROUPS, SSM_STATE, SSM_GROUP), (2 * SSM_GROUP) ** -0.5),
        "ssm_b_im": nrm((N_SSM, SSM_GROUPS, SSM_STATE, SSM_GROUP), (2 * SSM_GROUP) ** -0.5),
        "ssm_c_re": nrm((N_SSM, SSM_GROUPS, SSM_GROUP, SSM_STATE), (2 * SSM_STATE) ** -0.5),
        "ssm_c_im": nrm((N_SSM, SSM_GROUPS, SSM_GROUP, SSM_STATE), (2 * SSM_STATE) ** -0.5),
        "ssm_d": nrm((N_SSM, D_INNER), 1.0),
        "ssm_log_dt": jax.random.uniform(nk(), (N_SSM, SSM_GROUPS), F32, math.log(DT_MIN), math.log(DT_MAX)),
        "ssm_w_glu": nrm((N_SSM, D_INNER, D_INNER), e_in),
        "ssm_b_glu": nrm((N_SSM, D_INNER), 0.02),
        "ssm_w_out": nrm((N_SSM, D_INNER, D_MODEL), e_in * DEEPNORM_BETA),
        "attn_w_in": nrm((N_ATTN, D_MODEL, 4 * D_INNER), d_in),
        "attn_lam_q1": nrm((N_ATTN, DIFF_HEAD_DIM), 0.1),
        "attn_lam_k1": nrm((N_ATTN, DIFF_HEAD_DIM), 0.1),
        "attn_lam_q2": nrm((N_ATTN, DIFF_HEAD_DIM), 0.1),
        "attn_lam_k2": nrm((N_ATTN, DIFF_HEAD_DIM), 0.1),
        "attn_subln": 1.0 + nrm((N_ATTN, DIFF_ROW), 0.02),
        "attn_w_out": nrm((N_ATTN, D_INNER, D_MODEL), e_in * DEEPNORM_BETA),
        "hgrn_w_in": nrm((N_HGRN, D_MODEL, 4 * D_INNER), d_in),
        "hgrn_lb_logits": nrm((DEPTH, HGRN_HEADS * HGRN_DK), 0.1),
        "hgrn_g_norm": 1.0 + nrm((N_HGRN, HGRN_HEADS * HGRN_DV), 0.02),
        "hgrn_w_out": nrm((N_HGRN, D_INNER, D_MODEL), e_in * DEEPNORM_BETA),
    }


def reference(x_prompt, x_sample, cache_attn_k, cache_attn_v, state_ssm_re, state_ssm_im, state_hgrn,
              page_table, c_prompt, c_sample,
              ada_w, ada_b, ln_g, ln_b,
              ssm_w_in, ssm_a_re, ssm_a_im, ssm_b_re, ssm_b_im, ssm_c_re, ssm_c_im, ssm_d, ssm_log_dt,
              ssm_w_glu, ssm_b_glu, ssm_w_out,
              attn_w_in, attn_lam_q1, attn_lam_k1, attn_lam_q2, attn_lam_k2, attn_subln, attn_w_out,
              hgrn_w_in, hgrn_lb_logits, hgrn_g_norm, hgrn_w_out):
    lb_cum = jnp.cumsum(jax.nn.softmax(hgrn_lb_logits.astype(F32), axis=0), axis=0)
    lb_all = lb_cum - lb_cum[0]

    def run_group(x, c, pos, sample):
        n, L, _ = x.shape
        ssm_re, ssm_im, kv_k, kv_v, hgrn_s = [], [], [], [], []
        for i in range(DEPTH):
            mod = jax.nn.silu(c) @ ada_w[i] + ada_b[i]
            shift, scale, gate = jnp.split(mod[:, None, :], 3, axis=-1)
            h = x * (1 + scale) + shift
            if i % N_MIXERS == 0:
                j = SSM_LAYERS.index(i)
                if sample:
                    h0r, h0i = state_ssm_re[:, j], state_ssm_im[:, j]
                else:
                    h0r = jnp.zeros((n, SSM_GROUPS, SSM_STATE), x.dtype)
                    h0i = jnp.zeros((n, SSM_GROUPS, SSM_STATE), x.dtype)
                out, hr, hi = s5_branch(h, h0r, h0i, ssm_w_in[j], ssm_a_re[j], ssm_a_im[j], ssm_b_re[j],
                                        ssm_b_im[j], ssm_c_re[j], ssm_c_im[j], ssm_d[j], ssm_log_dt[j],
                                        ssm_w_glu[j], ssm_b_glu[j], ssm_w_out[j])
                ssm_re.append(hr)
                ssm_im.append(hi)
            elif i % N_MIXERS == 1:
                j = ATTN_LAYERS.index(i)
                lam_init = 0.8 - 0.6 * math.exp(-0.3 * i)
                lam = (jnp.exp(jnp.sum(attn_lam_q1[j].astype(F32) * attn_lam_k1[j].astype(F32)))
                       - jnp.exp(jnp.sum(attn_lam_q2[j].astype(F32) * attn_lam_k2[j].astype(F32)))
                       + lam_init)
                q, k, v, z = diff_attn_project(h, pos, attn_w_in[j])
                if sample:
                    k_past = cache_attn_k[page_table, j].reshape(n, -1, DIFF_HEADS, DIFF_ROW)
                    v_past = cache_attn_v[page_table, j].reshape(n, -1, DIFF_HEADS, DIFF_ROW)
                    a = diff_attn_cached(q, k, v, k_past, v_past, lam)
                else:
                    a = diff_attn_prompt(q, k, v, lam)
                a = rms_norm(a, attn_subln[j]) * (1.0 - lam_init)
                out = (a.reshape(n, L, D_INNER) * jax.nn.silu(z)) @ attn_w_out[j]
                kv_k.append(k)
                kv_v.append(v)
            else:
                j = HGRN_LAYERS.index(i)
                if sample:
                    s0 = state_hgrn[:, j]
                else:
                    s0 = jnp.zeros((n, HGRN_HEADS, HGRN_DK, HGRN_DV), x.dtype)
                out, s = hgrn2_branch(h, s0, hgrn_w_in[j], lb_all[i], hgrn_g_norm[j], hgrn_w_out[j])
                hgrn_s.append(s)
            x = layer_norm(DEEPNORM_ALPHA * x + (1 + gate) * out, ln_g[i], ln_b[i])
        return (x, jnp.stack(kv_k, axis=2), jnp.stack(kv_v, axis=2),
                jnp.stack(ssm_re, axis=1), jnp.stack(ssm_im, axis=1), jnp.stack(hgrn_s, axis=1))

    pos_p = jnp.arange(x_prompt.shape[1], dtype=jnp.int32)
    pos_s = PAST_LEN + jnp.arange(x_sample.shape[1], dtype=jnp.int32)
    y_p, k_p, v_p, sr_p, si_p, hs_p = run_group(x_prompt, c_prompt, pos_p, False)
    y_s, k_s, v_s, sr_s, si_s, hs_s = run_group(x_sample, c_sample, pos_s, True)
    return (y_p, y_s, k_p, v_p, sr_p, si_p, hs_p, k_s, v_s, sr_s, si_s, hs_s)
```

```python
import functools
import math

import jax
import jax.numpy as jnp
from jax import lax
from jax.experimental import pallas as pl
from jax.experimental.pallas import tpu as pltpu

F32 = jnp.float32
BF16 = jnp.bfloat16

LANES = 128
SUBLANES = 8
VMEM_LIMIT_BYTES = 56 * 1024 * 1024

N_MIXERS = 3
SSM_GROUP = 16
SSM_STATE = 64
DIFF_HEAD_DIM = 64
DIFF_ROW = 2 * DIFF_HEAD_DIM
ROPE_DIM = DIFF_HEAD_DIM // 4
ROPE_THETA = 500000.0
LN_EPS = 1e-5
RMS_EPS = 1e-6

S5_TILE_GROUPS = LANES // SSM_GROUP
S5_TILE_STATES = S5_TILE_GROUPS * SSM_STATE


def _params(*sem):
    return pltpu.CompilerParams(dimension_semantics=sem, vmem_limit_bytes=VMEM_LIMIT_BYTES)


def _silu(x):
    return x * jax.nn.sigmoid(x)


def _ada_kernel(c_ref, w_ref, b_ref, o_ref):
    s = _silu(c_ref[...]).astype(BF16)
    o_ref[...] = jnp.dot(s, w_ref[...].astype(BF16), preferred_element_type=F32) + b_ref[...]


def ada_modulation(c_all, ada_w, ada_b):
    depth, d, d3 = ada_w.shape
    rows = c_all.shape[0]
    tn = 1024
    return pl.pallas_call(
        _ada_kernel,
        grid=(depth, d3 // tn),
        in_specs=[
            pl.BlockSpec((rows, d), lambda i, j: (0, 0)),
            pl.BlockSpec((None, d, tn), lambda i, j: (i, 0, j)),
            pl.BlockSpec((None, 1, tn), lambda i, j: (i, 0, j)),
        ],
        out_specs=pl.BlockSpec((None, rows, tn), lambda i, j: (i, 0, j)),
        out_shape=jax.ShapeDtypeStruct((depth, rows, d3), F32),
        compiler_params=_params("parallel", "parallel"),
        name="ada_modulation",
    )(c_all, ada_w, ada_b.reshape(depth, 1, d3))


def _inproj_kernel(x_ref, shift_ref, scale_ref, w_ref, o_ref):
    h = x_ref[...] * (1.0 + scale_ref[...]) + shift_ref[...]
    bb, tl, d = h.shape
    o = jnp.dot(h.reshape(bb * tl, d).astype(BF16), w_ref[...], preferred_element_type=F32)
    o_ref[...] = o.reshape(bb, tl, o.shape[-1])


def mod_inproj(x, mod, w_bf16, bb, tl, tn=1024):
    n, L, d = x.shape
    nout = w_bf16.shape[1]
    return pl.pallas_call(
        _inproj_kernel,
        grid=(n // bb, L // tl, nout // tn),
        in_specs=[
            pl.BlockSpec((bb, tl, d), lambda b, l, j: (b, l, 0)),
            pl.BlockSpec((bb, 1, d), lambda b, l, j: (b, 0, 0)),
            pl.BlockSpec((bb, 1, d), lambda b, l, j: (b, 0, 1)),
            pl.BlockSpec((d, tn), lambda b, l, j: (0, j)),
        ],
        out_specs=pl.BlockSpec((bb, tl, tn), lambda b, l, j: (b, l, j)),
        out_shape=jax.ShapeDtypeStruct((n, L, nout), F32),
        compiler_params=_params("parallel", "parallel", "arbitrary"),
        name="mod_inproj",
    )(x, mod, mod, w_bf16)


def _out_ln_kernel(a_ref, z_ref, x_ref, gate_ref, w_ref, g_ref, b_ref, o_ref, *, alpha):
    act = a_ref[...] * _silu(z_ref[...])
    bb, tl, e = act.shape
    out = jnp.dot(act.reshape(bb * tl, e).astype(BF16), w_ref[...], preferred_element_type=F32)
    out = out.reshape(bb, tl, out.shape[-1])
    r = alpha * x_ref[...] + (1.0 + gate_ref[...]) * out
    mu = jnp.mean(r, axis=-1, keepdims=True)
    dlt = r - mu
    var = jnp.mean(dlt * dlt, axis=-1, keepdims=True)
    o_ref[...] = dlt * lax.rsqrt(var + LN_EPS) * g_ref[...] + b_ref[...]


def out_ln(a, zsrc, zblk, x, mod, w_bf16, ln_g, ln_b, alpha, bb, tl):
    n, L, d = x.shape
    e = a.shape[-1]
    return pl.pallas_call(
        functools.partial(_out_ln_kernel, alpha=alpha),
        grid=(n // bb, L // tl),
        in_specs=[
            pl.BlockSpec((bb, tl, e), lambda b, l: (b, l, 0)),
            pl.BlockSpec((bb, tl, e), lambda b, l: (b, l, zblk)),
            pl.BlockSpec((bb, tl, d), lambda b, l: (b, l, 0)),
            pl.BlockSpec((bb, 1, d), lambda b, l: (b, 0, 2)),
            pl.BlockSpec((e, d), lambda b, l: (0, 0)),
            pl.BlockSpec((1, d), lambda b, l: (0, 0)),
            pl.BlockSpec((1, d), lambda b, l: (0, 0)),
        ],
        out_specs=pl.BlockSpec((bb, tl, d), lambda b, l: (b, l, 0)),
        out_shape=jax.ShapeDtypeStruct((n, L, d), F32),
        compiler_params=_params("parallel", "parallel"),
        name="out_ln",
    )(a, zsrc, x, mod, w_bf16, ln_g.reshape(1, d), ln_b.reshape(1, d))


def _s5_scan_kernel(u_ref, bw_ref, cw_ref, coef_ref, h0r_ref, h0i_ref, y_ref, hr_ref, hi_ref,
                    bu_scr, car_scr, *, bs, t):
    c = pl.program_id(2)
    w = S5_TILE_STATES
    u = u_ref[...].reshape(bs * t, LANES).astype(BF16)
    bu_scr[...] = jnp.dot(u, bw_ref[...], preferred_element_type=F32)

    @pl.when(c == 0)
    def _():
        car_scr[:, :w] = jnp.broadcast_to(h0r_ref[...], (bs, SUBLANES, w)).reshape(bs * SUBLANES, w)
        car_scr[:, w:] = jnp.broadcast_to(h0i_ref[...], (bs, SUBLANES, w)).reshape(bs * SUBLANES, w)

    def seq_body(s, _):
        c0 = pl.multiple_of(s * SUBLANES, SUBLANES)
        carry = (car_scr[pl.ds(c0, SUBLANES), :w], car_scr[pl.ds(c0, SUBLANES), w:])

        def t_body(i, carry):
            cr, ci = carry
            r0 = pl.multiple_of(s * t + i * SUBLANES, SUBLANES)
            xr = bu_scr[pl.ds(r0, SUBLANES), :w]
            xi = bu_scr[pl.ds(r0, SUBLANES), w:]
            for step, k in enumerate((1, 2, 4)):
                ar = coef_ref[2 * step]
                ai = coef_ref[2 * step + 1]
                sr = pltpu.roll(xr, k, 0)
                si = pltpu.roll(xi, k, 0)
                xr, xi = xr + ar * sr - ai * si, xi + ar * si + ai * sr
            pr = coef_ref[6]
            pi = coef_ref[7]
            xr, xi = xr + pr * cr - pi * ci, xi + pr * ci + pi * cr
            bu_scr[pl.ds(r0, SUBLANES), :w] = xr
            bu_scr[pl.ds(r0, SUBLANES), w:] = xi
            last = SUBLANES - 1
            return (jnp.broadcast_to(xr[last:, :], (SUBLANES, w)),
                    jnp.broadcast_to(xi[last:, :], (SUBLANES, w)))

        cr, ci = lax.fori_loop(0, t // SUBLANES, t_body, carry)
        car_scr[pl.ds(c0, SUBLANES), :w] = cr
        car_scr[pl.ds(c0, SUBLANES), w:] = ci
        return 0

    lax.fori_loop(0, bs, seq_body, 0)

    y = jnp.dot(bu_scr[...].astype(BF16), cw_ref[...], preferred_element_type=F32)
    y_ref[...] = y.reshape(bs, t, LANES)

    @pl.when(c == pl.num_programs(2) - 1)
    def _():
        fin = car_scr[...].reshape(bs, SUBLANES, 2 * w)
        hr_ref[...] = fin[:, 0:1, :w]
        hi_ref[...] = fin[:, 0:1, w:]


def s5_scan(uz, bw, cw, coef, h0r, h0i, bs, t):
    n, L, _ = uz.shape
    ntile = bw.shape[0]
    w = S5_TILE_STATES
    st_spec = pl.BlockSpec((bs, None, 1, w), lambda b, j, c: (b, j, 0, 0))
    return pl.pallas_call(
        functools.partial(_s5_scan_kernel, bs=bs, t=t),
        grid=(n // bs, ntile, L // t),
        in_specs=[
            pl.BlockSpec((bs, t, LANES), lambda b, j, c: (b, c, j)),
            pl.BlockSpec((None, LANES, 2 * w), lambda b, j, c: (j, 0, 0)),
            pl.BlockSpec((None, 2 * w, LANES), lambda b, j, c: (j, 0, 0)),
            pl.BlockSpec((None, 8, SUBLANES, w), lambda b, j, c: (j, 0, 0, 0)),
            st_spec, st_spec,
        ],
        out_specs=[
            pl.BlockSpec((bs, t, LANES), lambda b, j, c: (b, c, j)),
            st_spec, st_spec,
        ],
        out_shape=[
            jax.ShapeDtypeStruct((n, L, ntile * LANES), F32),
            jax.ShapeDtypeStruct((n, ntile, 1, w), F32),
            jax.ShapeDtypeStruct((n, ntile, 1, w), F32),
        ],
        scratch_shapes=[
            pltpu.VMEM((bs * t, 2 * w), F32),
            pltpu.VMEM((bs * SUBLANES, 2 * w), F32),
        ],
        compiler_params=_params("parallel", "parallel", "arbitrary"),
        name="s5_scan",
    )(uz, bw, cw, coef, h0r, h0i)


def _s5_glu_kernel(y_ref, u_ref, d_ref, w_ref, b_ref, o_ref):
    y = jax.nn.gelu(y_ref[...] + d_ref[...] * u_ref[...])
    bb, tl, e = y.shape
    y2 = y.reshape(bb * tl, e)
    g = jnp.dot(y2.astype(BF16), w_ref[...], preferred_element_type=F32) + b_ref[...]
    o_ref[...] = (y2 * jax.nn.sigmoid(g)).reshape(bb, tl, e)


def s5_glu(y, uz, d_skip, w_glu_bf16, b_glu, bb, tl):
    n, L, e = y.shape
    return pl.pallas_call(
        _s5_glu_kernel,
        grid=(n // bb, L // tl),
        in_specs=[
            pl.BlockSpec((bb, tl, e), lambda b, l: (b, l, 0)),
            pl.BlockSpec((bb, tl, e), lambda b, l: (b, l, 0)),
            pl.BlockSpec((1, e), lambda b, l: (0, 0)),
            pl.BlockSpec((e, e), lambda b, l: (0, 0)),
            pl.BlockSpec((1, e), lambda b, l: (0, 0)),
        ],
        out_specs=pl.BlockSpec((bb, tl, e), lambda b, l: (b, l, 0)),
        out_shape=jax.ShapeDtypeStruct((n, L, e), F32),
        compiler_params=_params("parallel", "parallel"),
        name="s5_glu",
    )(y, uz, d_skip.reshape(1, e), w_glu_bf16, b_glu.reshape(1, e))


def s5_tables(a_re, a_im, b_re, b_im, c_re, c_im, log_dt):
    g, p = a_re.shape
    ntile = g // S5_TILE_GROUPS
    dt = jnp.exp(log_dt.astype(F32))[:, None]
    lr, li = a_re.astype(F32), a_im.astype(F32)
    mag = jnp.exp(lr * dt)
    lb_re, lb_im = mag * jnp.cos(li * dt), mag * jnp.sin(li * dt)
    nr, ni = lb_re - 1.0, lb_im
    den = lr * lr + li * li
    w_re = (nr * lr + ni * li) / den
    w_im = (ni * lr - nr * li) / den
    br, bi = b_re.astype(F32), b_im.astype(F32)
    bb_re = w_re[..., None] * br - w_im[..., None] * bi
    bb_im = w_re[..., None] * bi + w_im[..., None] * br

    eye = jnp.eye(S5_TILE_GROUPS, dtype=F32)

    def in_layout(bb):
        t4 = bb.reshape(ntile, S5_TILE_GROUPS, p, SSM_GROUP)
        return jnp.einsum('jgpc,gh->jgchp', t4, eye).reshape(ntile, LANES, S5_TILE_STATES)

    def out_layout(cc):
        t4 = cc.reshape(ntile, S5_TILE_GROUPS, SSM_GROUP, p)
        return jnp.einsum('jgcp,gh->jgphc', t4, eye).reshape(ntile, S5_TILE_STATES, LANES)

    bw = jnp.concatenate([in_layout(bb_re), in_layout(bb_im)], axis=2).astype(BF16)
    cw = jnp.concatenate([out_layout(c_re.astype(F32)), -out_layout(c_im.astype(F32))], axis=1).astype(BF16)

    def power(k):
        m = jnp.exp(k * lr * dt)
        return ((m * jnp.cos(k * li * dt)).reshape(ntile, 1, S5_TILE_STATES),
                (m * jnp.sin(k * li * dt)).reshape(ntile, 1, S5_TILE_STATES))

    rows = jnp.arange(SUBLANES)[None, :, None]
    coefs = []
    for k in (1, 2, 4):
        pr, pi = power(float(k))
        coefs += [jnp.where(rows >= k, pr, 0.0), jnp.where(rows >= k, pi, 0.0)]
    steps = (jnp.arange(SUBLANES, dtype=F32) + 1.0)[None, :, None]
    lrt = (lr * dt).reshape(ntile, 1, S5_TILE_STATES)
    lit = (li * dt).reshape(ntile, 1, S5_TILE_STATES)
    m = jnp.exp(steps * lrt)
    coefs += [m * jnp.cos(steps * lit), m * jnp.sin(steps * lit)]
    coef = jnp.stack(coefs, axis=1)
    return bw, cw, coef


def _rope_kernel(q_ref, k_ref, v_ref, cos_ref, sa_ref, sb_ref, qo_ref, kf_ref, kb_ref, vb_ref, *, qscale):
    cos = cos_ref[...][None]
    sa = sa_ref[...][None]
    sb = sb_ref[...][None]
    bb, tl, e = q_ref.shape

    def rot(x):
        x2 = x.reshape(bb * tl, LANES)
        up = pltpu.roll(x2, LANES - SUBLANES, 1).reshape(bb, tl, LANES)
        dn = pltpu.roll(x2, SUBLANES, 1).reshape(bb, tl, LANES)
        return x * cos + up * sa + dn * sb

    for tile in range(e // LANES):
        sl = slice(tile * LANES, (tile + 1) * LANES)
        qo_ref[:, :, sl] = (rot(q_ref[:, :, sl]) * qscale).astype(BF16)
        kr = rot(k_ref[:, :, sl])
        kf_ref[:, :, sl] = kr
        kb_ref[:, :, sl] = kr.astype(BF16)
    vb_ref[...] = v_ref[...].astype(BF16)


def rope_tables(pos):
    half = ROPE_DIM // 2
    assert half == SUBLANES
    inv = ROPE_THETA ** (-jnp.arange(half, dtype=F32) / half)
    ang = pos.astype(F32)[:, None] * inv[None, :]
    cos, sin = jnp.cos(ang), jnp.sin(ang)
    lane = jnp.arange(LANES) % DIFF_HEAD_DIM
    idx = lane % half
    first = (lane < half)[None, :]
    second = ((lane >= half) & (lane < ROPE_DIM))[None, :]
    cos_t = jnp.where(first | second, cos[:, idx], 1.0)
    sa_t = jnp.where(first, -sin[:, idx], 0.0)
    sb_t = jnp.where(second, sin[:, idx], 0.0)
    return cos_t, sa_t, sb_t


def rope_qkv(qkvz, tables, bb, tl):
    n, L, e4 = qkvz.shape
    e = e4 // 4
    cos_t, sa_t, sb_t = tables
    tab = pl.BlockSpec((tl, LANES), lambda b, l: (l, 0))
    blk = lambda col: pl.BlockSpec((bb, tl, e), lambda b, l: (b, l, col))
    return pl.pallas_call(
        functools.partial(_rope_kernel, qscale=DIFF_HEAD_DIM ** -0.5),
        grid=(n // bb, L // tl),
        in_specs=[blk(0), blk(1), blk(2), tab, tab, tab],
        out_specs=[blk(0)] * 4,
        out_shape=[
            jax.ShapeDtypeStruct((n, L, e), BF16),
            jax.ShapeDtypeStruct((n, L, e), F32),
            jax.ShapeDtypeStruct((n, L, e), BF16),
            jax.ShapeDtypeStruct((n, L, e), BF16),
        ],
        compiler_params=_params("parallel", "parallel"),
        name="rope_qkv",
    )(qkvz, qkvz, qkvz, cos_t, sa_t, sb_t)


def _flash_kernel(lam_ref, q_ref, k_ref, v_ref, g_ref, o_ref, m_scr, l_scr, acc_scr, *, tq, tk, post_scale):
    qi = pl.program_id(2)
    ki = pl.program_id(3)

    @pl.when(ki == 0)
    def _():
        m_scr[...] = jnp.full(m_scr.shape, -jnp.inf, F32)
        l_scr[...] = jnp.zeros(l_scr.shape, F32)
        acc_scr[...] = jnp.zeros(acc_scr.shape, F32)

    @pl.when(ki <= qi)
    def _():
        q = q_ref[...]
        k = k_ref[...]
        v = v_ref[...]
        lane = lax.broadcasted_iota(jnp.int32, q.shape, 1)
        rows = qi * tq + lax.broadcasted_iota(jnp.int32, (tq, tk), 0)
        cols = ki * tk + lax.broadcasted_iota(jnp.int32, (tq, tk), 1)
        causal = rows >= cols
        for j in range(2):
            sel = (lane < DIFF_HEAD_DIM) if j == 0 else (lane >= DIFF_HEAD_DIM)
            qj = jnp.where(sel, q, jnp.zeros_like(q))
            s = lax.dot_general(qj, k, (((1,), (1,)), ((), ())), preferred_element_type=F32)
            s = jnp.where(causal, s, -jnp.inf)
            m_prev = m_scr[j]
            m_new = jnp.maximum(m_prev, jnp.max(s, axis=1, keepdims=True))
            alpha = jnp.exp(m_prev - m_new)
            p = jnp.exp(s - m_new)
            l_scr[j] = alpha * l_scr[j] + jnp.sum(p, axis=1, keepdims=True)
            acc_scr[j] = alpha * acc_scr[j] + jnp.dot(p.astype(BF16), v, preferred_element_type=F32)
            m_scr[j] = m_new

    @pl.when(ki == qi)
    def _():
        lam = lam_ref[0, 0]
        a = acc_scr[0] / l_scr[0] - lam * (acc_scr[1] / l_scr[1])
        ms = jnp.mean(a * a, axis=-1, keepdims=True)
        o_ref[...] = a * lax.rsqrt(ms + RMS_EPS) * g_ref[...] * post_scale


def flash_diff_attn(q, k, v, lam, subln, post_scale, tq):
    n, L, e = q.shape
    heads = e // DIFF_ROW
    tk = tq
    qspec = pl.BlockSpec((None, tq, DIFF_ROW), lambda b, h, i, j: (b, i, h))
    kspec = pl.BlockSpec((None, tk, DIFF_ROW), lambda b, h, i, j: (b, jnp.minimum(i, j), h))
    return pl.pallas_call(
        functools.partial(_flash_kernel, tq=tq, tk=tk, post_scale=post_scale),
        grid=(n, heads, L // tq, L // tk),
        in_specs=[
            pl.BlockSpec(memory_space=pltpu.SMEM),
            qspec, kspec, kspec,
            pl.BlockSpec((1, DIFF_ROW), lambda b, h, i, j: (0, 0)),
        ],
        out_specs=qspec,
        out_shape=jax.ShapeDtypeStruct((n, L, e), F32),
        scratch_shapes=[
            pltpu.VMEM((2, tq, 1), F32),
            pltpu.VMEM((2, tq, 1), F32),
            pltpu.VMEM((2, tq, DIFF_ROW), F32),
        ],
        compiler_params=_params("parallel", "parallel", "parallel", "arbitrary"),
        name="flash_diff_attn",
    )(lam.reshape(1, 1), q, k, v, subln.reshape(1, DIFF_ROW))


PAIR_ROWS = 4 * SUBLANES
PAIR_COLS = 2 * DIFF_ROW


def _cached_attn_kernel(pt_ref, lam_ref, qd_ref, kn_ref, vn_ref, kc_ref, vc_ref, g_ref, o_ref,
                        m_scr, l_scr, acc_scr, *, npairs, post_scale):
    del pt_ref
    p = pl.program_id(1)
    nt = SUBLANES
    contract_last = (((1,), (1,)), ((), ()))

    def scores(key_tile):
        parts = [lax.dot_general(qd_ref[hp], key_tile(hp), contract_last, preferred_element_type=F32)
                 for hp in range(npairs)]
        return jnp.concatenate(parts, axis=0)

    def pair_slice(hp):
        return slice(hp * PAIR_ROWS, (hp + 1) * PAIR_ROWS)

    @pl.when(p == 0)
    def _():
        s = scores(lambda hp: kn_ref[:, hp * PAIR_COLS:(hp + 1) * PAIR_COLS])
        qpos = lax.broadcasted_iota(jnp.int32, s.shape, 0) % nt
        kpos = lax.broadcasted_iota(jnp.int32, s.shape, 1)
        s = jnp.where(kpos <= qpos, s, -jnp.inf)
        m = jnp.max(s, axis=1, keepdims=True)
        pr = jnp.exp(s - m)
        m_scr[...] = m
        l_scr[...] = jnp.sum(pr, axis=1, keepdims=True)
        prb = pr.astype(BF16)
        for hp in range(npairs):
            acc_scr[hp] = jnp.dot(prb[pair_slice(hp)], vn_ref[:, hp * PAIR_COLS:(hp + 1) * PAIR_COLS],
                                  preferred_element_type=F32)

    def page_tile(ref, hp):
        return jnp.concatenate([ref[:, 2 * hp, :], ref[:, 2 * hp + 1, :]], axis=1).astype(BF16)

    s = scores(lambda hp: page_tile(kc_ref, hp))
    m_prev = m_scr[...]
    m_new = jnp.maximum(m_prev, jnp.max(s, axis=1, keepdims=True))
    alpha = jnp.exp(m_prev - m_new)
    pr = jnp.exp(s - m_new)
    l_scr[...] = alpha * l_scr[...] + jnp.sum(pr, axis=1, keepdims=True)
    m_scr[...] = m_new
    prb = pr.astype(BF16)
    for hp in range(npairs):
        acc_scr[hp] = alpha[pair_slice(hp)] * acc_scr[hp] + jnp.dot(
            prb[pair_slice(hp)], page_tile(vc_ref, hp), preferred_element_type=F32)

    @pl.when(p == pl.num_programs(1) - 1)
    def _():
        lam = lam_ref[0, 0]
        l_all = l_scr[...]
        for hp in range(npairs):
            acc = acc_scr[hp]
            for hh in range(2):
                r0 = hp * PAIR_ROWS + hh * 2 * nt
                a0 = acc[hh * 2 * nt:hh * 2 * nt + nt, hh * DIFF_ROW:(hh + 1) * DIFF_ROW]
                a1 = acc[hh * 2 * nt + nt:(hh + 1) * 2 * nt, hh * DIFF_ROW:(hh + 1) * DIFF_ROW]
                a = a0 / l_all[r0:r0 + nt] - lam * (a1 / l_all[r0 + nt:r0 + 2 * nt])
                ms = jnp.mean(a * a, axis=-1, keepdims=True)
                h = 2 * hp + hh
                o_ref[:, h * DIFF_ROW:(h + 1) * DIFF_ROW] = a * lax.rsqrt(ms + RMS_EPS) * g_ref[...] * post_scale


def cached_diff_attn(q_bf, k_bf, v_bf, cache_k, cache_v, layer_j, page_table, lam, subln, post_scale):
    n, t, e = q_bf.shape
    assert t == SUBLANES
    heads = e // DIFF_ROW
    npairs = heads // 2
    n_pages = page_table.shape[1]
    page = cache_k.shape[2]
    q5 = q_bf.reshape(n, t, npairs, 2, 2, DIFF_HEAD_DIM)
    eye2 = jnp.eye(2, dtype=BF16)
    qd = jnp.einsum('ntphjd,hg,jk->nphjtgkd', q5, eye2, eye2).reshape(n, npairs, PAIR_ROWS, PAIR_COLS)
    pad = ((0, 0), (0, 2 * SUBLANES - t), (0, 0))
    kn = jnp.pad(k_bf, pad)
    vn = jnp.pad(v_bf, pad)
    cache_spec = pl.BlockSpec((None, None, page, heads, DIFF_ROW),
                              lambda b, p, pt: (pt[b, p], layer_j, 0, 0, 0))
    new_spec = pl.BlockSpec((None, 2 * SUBLANES, e), lambda b, p, pt: (b, 0, 0))
    grid_spec = pltpu.PrefetchScalarGridSpec(
        num_scalar_prefetch=1,
        grid=(n, n_pages),
        in_specs=[
            pl.BlockSpec(memory_space=pltpu.SMEM),
            pl.BlockSpec((None, npairs, PAIR_ROWS, PAIR_COLS), lambda b, p, pt: (b, 0, 0, 0)),
            new_spec, new_spec, cache_spec, cache_spec,
            pl.BlockSpec((1, DIFF_ROW), lambda b, p, pt: (0, 0)),
        ],
        out_specs=pl.BlockSpec((None, t, e), lambda b, p, pt: (b, 0, 0)),
        scratch_shapes=[
            pltpu.VMEM((npairs * PAIR_ROWS, 1), F32),
            pltpu.VMEM((npairs * PAIR_ROWS, 1), F32),
            pltpu.VMEM((npairs, PAIR_ROWS, PAIR_COLS), F32),
        ],
    )
    return pl.pallas_call(
        functools.partial(_cached_attn_kernel, npairs=npairs, post_scale=post_scale),
        grid_spec=grid_spec,
        out_shape=jax.ShapeDtypeStruct((n, t, e), F32),
        compiler_params=_params("parallel", "arbitrary"),
        name="cached_diff_attn",
    )(page_table, lam.reshape(1, 1), qd, kn, vn, cache_k, cache_v, subln.reshape(1, DIFF_ROW))


def _hgrn_kernel(q_ref, fz_ref, v_ref, lb_ref, g_ref, s0_ref, o_ref, sout_ref, st_scr, *, tc):
    c = pl.program_id(2)

    @pl.when(c == 0)
    def _():
        st_scr[...] = s0_ref[...].T

    q = q_ref[...]
    fz = fz_ref[...]
    v = v_ref[...]
    lb = lb_ref[...]
    dk = q.shape[1]
    logf = jnp.log(lb + (1.0 - lb) * jax.nn.sigmoid(fz))
    kk = (1.0 - lb) * jax.nn.sigmoid(-fz)

    row = lax.broadcasted_iota(jnp.int32, (tc, dk), 0)
    b = logf
    k = 1
    while k < tc:
        b = b + jnp.where(row >= k, pltpu.roll(b, k, 0), 0.0)
        k *= 2

    contract_last = (((1,), (1,)), ((), ()))
    contract_first = (((0,), (0,)), ((), ()))
    st = st_scr[...]
    o = lax.dot_general((q * jnp.exp(b)).astype(BF16), st.astype(BF16), contract_last,
                        preferred_element_type=F32)

    rin = row % SUBLANES
    for d in range(SUBLANES):
        if d == 0:
            w = q * kk
            vr = v
        else:
            w = q * pltpu.roll(kk, d, 0) * jnp.exp(b - pltpu.roll(b, d, 0))
            vr = pltpu.roll(v, d, 0)
        cd = jnp.sum(jnp.where(rin >= d, w, 0.0), axis=1, keepdims=True)
        o = o + cd * vr

    if tc > SUBLANES:
        trow = lax.broadcasted_iota(jnp.int32, (tc, tc), 0)
        tcol = lax.broadcasted_iota(jnp.int32, (tc, tc), 1)
        att = jnp.zeros((tc, tc), F32)
        blk = 2 * SUBLANES
        while blk <= tc:
            half = blk // 2
            b3 = b.reshape(tc // blk, blk, dk)
            anchor = jnp.broadcast_to(b3[:, half - 1:half, :], b3.shape).reshape(tc, dk)
            pos = row % blk
            qt = jnp.where(pos >= half, q * jnp.exp(b - anchor), 0.0)
            kt = jnp.where(pos < half, kk * jnp.exp(anchor - b), 0.0)
            a = lax.dot_general(qt.astype(BF16), kt.astype(BF16), contract_last, preferred_element_type=F32)
            att = att + jnp.where((trow // blk) == (tcol // blk), a, 0.0)
            blk *= 2
        o = o + jnp.dot(att.astype(BF16), v.astype(BF16), preferred_element_type=F32)

    ms = jnp.mean(o * o, axis=-1, keepdims=True)
    o_ref[...] = o * lax.rsqrt(ms + RMS_EPS) * g_ref[...]

    b_end = b[tc - 1:, :]
    khat = kk * jnp.exp(b_end - b)
    st_new = st * jnp.exp(b_end) + lax.dot_general(v.astype(BF16), khat.astype(BF16), contract_first,
                                                   preferred_element_type=F32)
    st_scr[...] = st_new

    @pl.when(c == pl.num_programs(2) - 1)
    def _():
        sout_ref[...] = st_new.T


def hgrn_scan(qfvz, lb, g_norm, s0, tc):
    n, L, e4 = qfvz.shape
    e = e4 // 4
    heads = s0.shape[1]
    dk, dv = s0.shape[2], s0.shape[3]
    assert dk == LANES and dv == LANES
    col = lambda base: pl.BlockSpec((None, tc, dk), lambda b, h, c: (b, c, base * heads + h))
    vec = pl.BlockSpec((None, 1, dk), lambda b, h, c: (h, 0, 0))
    st = pl.BlockSpec((None, None, dk, dv), lambda b, h, c: (b, h, 0, 0))
    return pl.pallas_call(
        functools.partial(_hgrn_kernel, tc=tc),
        grid=(n, heads, L // tc),
        in_specs=[col(0), col(1), col(2), vec, vec, st],
        out_specs=[pl.BlockSpec((None, tc, dv), lambda b, h, c: (b, c, h)), st],
        out_shape=[jax.ShapeDtypeStruct((n, L, e), F32), jax.ShapeDtypeStruct(s0.shape, F32)],
        scratch_shapes=[pltpu.VMEM((dv, dk), F32)],
        compiler_params=_params("parallel", "parallel", "arbitrary"),
        name="hgrn_scan",
    )(qfvz, qfvz, qfvz, lb.reshape(heads, 1, dk), g_norm.reshape(heads, 1, dv), s0)


def kernel(x_prompt, x_sample, cache_attn_k, cache_attn_v, state_ssm_re, state_ssm_im, state_hgrn, page_table, c_prompt, c_sample, ada_w, ada_b, ln_g, ln_b, ssm_w_in, ssm_a_re, ssm_a_im, ssm_b_re, ssm_b_im, ssm_c_re, ssm_c_im, ssm_d, ssm_log_dt, ssm_w_glu, ssm_b_glu, ssm_w_out, attn_w_in, attn_lam_q1, attn_lam_k1, attn_lam_q2, attn_lam_k2, attn_subln, attn_w_out, hgrn_w_in, hgrn_lb_logits, hgrn_g_norm, hgrn_w_out):
    depth = ada_w.shape[0]
    d_model = x_prompt.shape[-1]
    alpha = (2 * depth) ** 0.25
    past_len = page_table.shape[1] * cache_attn_k.shape[2]
    ssm_layers = [i for i in range(depth) if i % N_MIXERS == 0]
    attn_layers = [i for i in range(depth) if i % N_MIXERS == 1]
    hgrn_layers = [i for i in range(depth) if i % N_MIXERS == 2]

    n_p, n_s = x_prompt.shape[0], x_sample.shape[0]
    rows = n_p + n_s
    rows_pad = -(-rows // SUBLANES) * SUBLANES
    c_all = jnp.pad(jnp.concatenate([c_prompt, c_sample], axis=0), ((0, rows_pad - rows), (0, 0)))
    mod_all = ada_modulation(c_all, ada_w, ada_b)

    lb_cum = jnp.cumsum(jax.nn.softmax(hgrn_lb_logits.astype(F32), axis=0), axis=0)
    lb_all = lb_cum - lb_cum[0]

    s5_tabs = [s5_tables(ssm_a_re[j], ssm_a_im[j], ssm_b_re[j], ssm_b_im[j], ssm_c_re[j], ssm_c_im[j],
                         ssm_log_dt[j]) for j in range(len(ssm_layers))]
    ntile = ssm_a_re.shape[1] // S5_TILE_GROUPS

    def run_group(x, mod_rows, pos, sample):
        n, L, _ = x.shape
        if sample:
            blk_in = dict(bb=min(n, 128), tl=L)
            blk_out = dict(bb=min(n, 32), tl=L)
            blk_glu = dict(bb=min(n, 32), tl=L)
            blk_rope = dict(bb=min(n, 32), tl=L)
            s5_blk = dict(bs=min(n, 32), t=L)
            hgrn_tc = L
        else:
            blk_in = dict(bb=1, tl=min(L, 1024))
            blk_out = dict(bb=1, tl=min(L, 256))
            blk_glu = dict(bb=1, tl=min(L, 256))
            blk_rope = dict(bb=1, tl=min(L, 256))
            s5_blk = dict(bs=1, t=min(L, 512))
            hgrn_tc = min(L, 128)
        rope_tabs = rope_tables(pos)
        ssm_re, ssm_im, kv_k, kv_v, hgrn_s = [], [], [], [], []
        for i in range(depth):
            mod = mod_all[i, mod_rows][:, None, :]
            if i % N_MIXERS == 0:
                j = ssm_layers.index(i)
                uz = mod_inproj(x, mod, ssm_w_in[j].astype(BF16), **blk_in)
                bw, cw, coef = s5_tabs[j]
                if sample:
                    h0r = state_ssm_re[:, j].astype(F32).reshape(n, ntile, 1, S5_TILE_STATES)
                    h0i = state_ssm_im[:, j].astype(F32).reshape(n, ntile, 1, S5_TILE_STATES)
                else:
                    h0r = jnp.zeros((n, ntile, 1, S5_TILE_STATES), F32)
                    h0i = h0r
                y, hr, hi = s5_scan(uz, bw, cw, coef, h0r, h0i, **s5_blk)
                a = s5_glu(y, uz, ssm_d[j], ssm_w_glu[j].astype(BF16), ssm_b_glu[j], **blk_glu)
                x = out_ln(a, uz, 1, x, mod, ssm_w_out[j].astype(BF16), ln_g[i], ln_b[i], alpha, **blk_out)
                ssm_re.append(hr.reshape(n, -1, SSM_STATE))
                ssm_im.append(hi.reshape(n, -1, SSM_STATE))
            elif i % N_MIXERS == 1:
                j = attn_layers.index(i)
                lam_init = 0.8 - 0.6 * math.exp(-0.3 * i)
                lam = (jnp.exp(jnp.sum(attn_lam_q1[j].astype(F32) * attn_lam_k1[j].astype(F32)))
                       - jnp.exp(jnp.sum(attn_lam_q2[j].astype(F32) * attn_lam_k2[j].astype(F32)))
                       + lam_init)
                qkvz = mod_inproj(x, mod, attn_w_in[j].astype(BF16), **blk_in)
                e = qkvz.shape[-1] // 4
                heads = e // DIFF_ROW
                if sample:
                    q_bf, k_f, k_bf, v_bf = rope_qkv(qkvz, rope_tabs, **blk_rope)
                    a = cached_diff_attn(q_bf, k_bf, v_bf, cache_attn_k, cache_attn_v, j, page_table, lam,
                                         attn_subln[j], 1.0 - lam_init)
                else:
                    q_bf, k_f, k_bf, v_bf = rope_qkv(qkvz, rope_tabs, **blk_rope)
                    a = flash_diff_attn(q_bf, k_bf, v_bf, lam, attn_subln[j], 1.0 - lam_init, tq=min(L, 512))
                x = out_ln(a, qkvz, 3, x, mod, attn_w_out[j].astype(BF16), ln_g[i], ln_b[i], alpha, **blk_out)
                kv_k.append(k_f.reshape(n, L, heads, DIFF_ROW))
                kv_v.append(qkvz[:, :, 2 * e:3 * e].reshape(n, L, heads, DIFF_ROW))
            else:
                j = hgrn_layers.index(i)
                qfvz = mod_inproj(x, mod, hgrn_w_in[j].astype(BF16), **blk_in)
                if sample:
                    s0 = state_hgrn[:, j].astype(F32)
                else:
                    s0 = jnp.zeros((n,) + state_hgrn.shape[2:], F32)
                o, s_fin = hgrn_scan(qfvz, lb_all[i], hgrn_g_norm[j].astype(F32), s0, hgrn_tc)
                x = out_ln(o, qfvz, 3, x, mod, hgrn_w_out[j].astype(BF16), ln_g[i], ln_b[i], alpha, **blk_out)
                hgrn_s.append(s_fin)
        return (x, jnp.stack(kv_k, axis=2), jnp.stack(kv_v, axis=2),
                jnp.stack(ssm_re, axis=1), jnp.stack(ssm_im, axis=1), jnp.stack(hgrn_s, axis=1))

    pos_p = jnp.arange(x_prompt.shape[1], dtype=jnp.int32)
    pos_s = past_len + jnp.arange(x_sample.shape[1], dtype=jnp.int32)
    y_p, k_p, v_p, sr_p, si_p, hs_p = run_group(x_prompt, slice(0, n_p), pos_p, False)
    y_s, k_s, v_s, sr_s, si_s, hs_s = run_group(x_sample, slice(n_p, n_p + n_s), pos_s, True)
    return (y_p, y_s, k_p, v_p, sr_p, si_p, hs_p, k_s, v_s, sr_s, si_s, hs_s)
```

```python
import functools
import math

import jax
import jax.numpy as jnp
from jax import lax
from jax.experimental import pallas as pl
from jax.experimental.pallas import tpu as pltpu

F32 = jnp.float32
BF16 = jnp.bfloat16

LANES = 128
SUBLANES = 8
BF16_SUBLANES = 16
VMEM_LIMIT_BYTES = 56 * 1024 * 1024

N_MIXERS = 3
SSM_GROUP = 16
SSM_STATE = 64
DIFF_HEAD_DIM = 64
DIFF_ROW = 2 * DIFF_HEAD_DIM
ROPE_DIM = DIFF_HEAD_DIM // 4
ROPE_THETA = 500000.0
LN_EPS = 1e-5
RMS_EPS = 1e-6
LOG2E = 1.4426950408889634

S5_TILE_GROUPS = 16
S5_TILE_CH = S5_TILE_GROUPS * SSM_GROUP
S5_TILE_STATES = S5_TILE_GROUPS * SSM_STATE
S5_CHUNKS = S5_TILE_STATES // LANES
assert S5_CHUNKS == SUBLANES

VT_ROWS = DIFF_ROW + BF16_SUBLANES


def _params(*sem):
    return pltpu.CompilerParams(dimension_semantics=sem, vmem_limit_bytes=VMEM_LIMIT_BYTES)


def _silu(x):
    return x * jax.nn.sigmoid(x)


def _ada_kernel(c_ref, w_ref, b_ref, o_ref):
    s = _silu(c_ref[...]).astype(BF16)
    o_ref[...] = jnp.dot(s, w_ref[...].astype(BF16), preferred_element_type=F32) + b_ref[...]


def ada_modulation(c_all, ada_w, ada_b):
    depth, d, d3 = ada_w.shape
    rows = c_all.shape[0]
    tn = 1024
    return pl.pallas_call(
        _ada_kernel,
        grid=(depth, d3 // tn),
        in_specs=[
            pl.BlockSpec((rows, d), lambda i, j: (0, 0)),
            pl.BlockSpec((None, d, tn), lambda i, j: (i, 0, j)),
            pl.BlockSpec((None, 1, tn), lambda i, j: (i, 0, j)),
        ],
        out_specs=pl.BlockSpec((None, rows, tn), lambda i, j: (i, 0, j)),
        out_shape=jax.ShapeDtypeStruct((depth, rows, d3), F32),
        compiler_params=_params("parallel", "parallel"),
        name="ada_modulation",
    )(c_all, ada_w, ada_b.reshape(depth, 1, d3))


def _inproj_kernel(x_ref, shift_ref, scale_ref, w_ref, o_ref):
    h = x_ref[...] * (1.0 + scale_ref[...]) + shift_ref[...]
    bb, tl, d = h.shape
    o = jnp.dot(h.reshape(bb * tl, d).astype(BF16), w_ref[...], preferred_element_type=F32)
    o_ref[...] = o.reshape(bb, tl, o.shape[-1])


def mod_inproj(x, mod, w_bf16, bb, tl, tn=1024):
    n, L, d = x.shape
    nout = w_bf16.shape[1]
    return pl.pallas_call(
        _inproj_kernel,
        grid=(n // bb, L // tl, nout // tn),
        in_specs=[
            pl.BlockSpec((bb, tl, d), lambda b, l, j: (b, l, 0)),
            pl.BlockSpec((bb, 1, d), lambda b, l, j: (b, 0, 0)),
            pl.BlockSpec((bb, 1, d), lambda b, l, j: (b, 0, 1)),
            pl.BlockSpec((d, tn), lambda b, l, j: (0, j)),
        ],
        out_specs=pl.BlockSpec((bb, tl, tn), lambda b, l, j: (b, l, j)),
        out_shape=jax.ShapeDtypeStruct((n, L, nout), F32),
        compiler_params=_params("parallel", "parallel", "arbitrary"),
        name="mod_inproj",
    )(x, mod, mod, w_bf16)


def _out_ln_kernel(a_ref, z_ref, x_ref, gate_ref, w_ref, g_ref, b_ref, o_ref, *, alpha):
    act = a_ref[...] * _silu(z_ref[...])
    bb, tl, e = act.shape
    out = jnp.dot(act.reshape(bb * tl, e).astype(BF16), w_ref[...], preferred_element_type=F32)
    out = out.reshape(bb, tl, out.shape[-1])
    r = alpha * x_ref[...] + (1.0 + gate_ref[...]) * out
    mu = jnp.mean(r, axis=-1, keepdims=True)
    dlt = r - mu
    var = jnp.mean(dlt * dlt, axis=-1, keepdims=True)
    o_ref[...] = dlt * lax.rsqrt(var + LN_EPS) * g_ref[...] + b_ref[...]


def out_ln(a, zsrc, zblk, x, mod, w_bf16, ln_g, ln_b, alpha, bb, tl):
    n, L, d = x.shape
    e = a.shape[-1]
    return pl.pallas_call(
        functools.partial(_out_ln_kernel, alpha=alpha),
        grid=(n // bb, L // tl),
        in_specs=[
            pl.BlockSpec((bb, tl, e), lambda b, l: (b, l, 0)),
            pl.BlockSpec((bb, tl, e), lambda b, l: (b, l, zblk)),
            pl.BlockSpec((bb, tl, d), lambda b, l: (b, l, 0)),
            pl.BlockSpec((bb, 1, d), lambda b, l: (b, 0, 2)),
            pl.BlockSpec((e, d), lambda b, l: (0, 0)),
            pl.BlockSpec((1, d), lambda b, l: (0, 0)),
            pl.BlockSpec((1, d), lambda b, l: (0, 0)),
        ],
        out_specs=pl.BlockSpec((bb, tl, d), lambda b, l: (b, l, 0)),
        out_shape=jax.ShapeDtypeStruct((n, L, d), F32),
        compiler_params=_params("parallel", "parallel"),
        name="out_ln",
    )(a, zsrc, x, mod, w_bf16, ln_g.reshape(1, d), ln_b.reshape(1, d))


def _s5_scan_kernel(u_ref, bw_ref, cw_ref, lam_ref, h0r_ref, h0i_ref, y_ref, hr_ref, hi_ref,
                    xr_scr, xi_scr, car_scr, *, bs, t):
    c = pl.program_id(2)
    nst = S5_TILE_STATES
    u = u_ref[...].reshape(bs * t, S5_TILE_CH).astype(BF16)
    bu = jnp.dot(u, bw_ref[...], preferred_element_type=F32)
    for s in range(bs):
        rows = slice(s * t, (s + 1) * t)
        for a in range(S5_CHUNKS):
            xr_scr[s, pl.ds(a, t, stride=SUBLANES), :] = bu[rows, a * LANES:(a + 1) * LANES]
            xi_scr[s, pl.ds(a, t, stride=SUBLANES), :] = bu[rows, nst + a * LANES:nst + (a + 1) * LANES]

    @pl.when(c == 0)
    def _():
        car_scr[0] = h0r_ref[...]
        car_scr[1] = h0i_ref[...]

    lr = lam_ref[0]
    li = lam_ref[1]

    def body(p, carry):
        r0 = pl.multiple_of(p * SUBLANES, SUBLANES)
        out = []
        for s in range(bs):
            hr, hi = carry[2 * s], carry[2 * s + 1]
            nr = lr * hr - li * hi + xr_scr[s, pl.ds(r0, SUBLANES), :]
            ni = lr * hi + li * hr + xi_scr[s, pl.ds(r0, SUBLANES), :]
            xr_scr[s, pl.ds(r0, SUBLANES), :] = nr
            xi_scr[s, pl.ds(r0, SUBLANES), :] = ni
            out += [nr, ni]
        return tuple(out)

    init = []
    for s in range(bs):
        init += [car_scr[0, s], car_scr[1, s]]
    fin = lax.fori_loop(0, t, body, tuple(init), unroll=SUBLANES)
    for s in range(bs):
        car_scr[0, s] = fin[2 * s]
        car_scr[1, s] = fin[2 * s + 1]

    for s in range(bs):
        parts = [xr_scr[s, pl.ds(a, t, stride=SUBLANES), :] for a in range(S5_CHUNKS)]
        parts += [xi_scr[s, pl.ds(a, t, stride=SUBLANES), :] for a in range(S5_CHUNKS)]
        hcat = jnp.concatenate(parts, axis=1).astype(BF16)
        y_ref[s] = jnp.dot(hcat, cw_ref[...], preferred_element_type=F32)

    @pl.when(c == pl.num_programs(2) - 1)
    def _():
        hr_ref[...] = car_scr[0]
        hi_ref[...] = car_scr[1]


def s5_scan(uz, bw, cw, lam, h0r, h0i, bs, t):
    n, L, _ = uz.shape
    ntile = bw.shape[0]
    nst = S5_TILE_STATES
    st_spec = pl.BlockSpec((bs, None, SUBLANES, LANES), lambda b, j, c: (b, j, 0, 0))
    return pl.pallas_call(
        functools.partial(_s5_scan_kernel, bs=bs, t=t),
        grid=(n // bs, ntile, L // t),
        in_specs=[
            pl.BlockSpec((bs, t, S5_TILE_CH), lambda b, j, c: (b, c, j)),
            pl.BlockSpec((None, S5_TILE_CH, 2 * nst), lambda b, j, c: (j, 0, 0)),
            pl.BlockSpec((None, 2 * nst, S5_TILE_CH), lambda b, j, c: (j, 0, 0)),
            pl.BlockSpec((None, 2, SUBLANES, LANES), lambda b, j, c: (j, 0, 0, 0)),
            st_spec, st_spec,
        ],
        out_specs=[
            pl.BlockSpec((bs, t, S5_TILE_CH), lambda b, j, c: (b, c, j)),
            st_spec, st_spec,
        ],
        out_shape=[
            jax.ShapeDtypeStruct((n, L, ntile * S5_TILE_CH), F32),
            jax.ShapeDtypeStruct((n, ntile, SUBLANES, LANES), F32),
            jax.ShapeDtypeStruct((n, ntile, SUBLANES, LANES), F32),
        ],
        scratch_shapes=[
            pltpu.VMEM((bs, t * SUBLANES, LANES), F32),
            pltpu.VMEM((bs, t * SUBLANES, LANES), F32),
            pltpu.VMEM((2, bs, SUBLANES, LANES), F32),
        ],
        compiler_params=_params("parallel", "parallel", "arbitrary"),
        name="s5_scan",
    )(uz, bw, cw, lam, h0r, h0i)


def _s5_glu_kernel(y_ref, u_ref, d_ref, w_ref, b_ref, o_ref):
    y = jax.nn.gelu(y_ref[...] + d_ref[...] * u_ref[...])
    bb, tl, e = y.shape
    y2 = y.reshape(bb * tl, e)
    g = jnp.dot(y2.astype(BF16), w_ref[...], preferred_element_type=F32) + b_ref[...]
    o_ref[...] = (y2 * jax.nn.sigmoid(g)).reshape(bb, tl, e)


def s5_glu(y, uz, d_skip, w_glu_bf16, b_glu, bb, tl):
    n, L, e = y.shape
    return pl.pallas_call(
        _s5_glu_kernel,
        grid=(n // bb, L // tl),
        in_specs=[
            pl.BlockSpec((bb, tl, e), lambda b, l: (b, l, 0)),
            pl.BlockSpec((bb, tl, e), lambda b, l: (b, l, 0)),
            pl.BlockSpec((1, e), lambda b, l: (0, 0)),
            pl.BlockSpec((e, e), lambda b, l: (0, 0)),
            pl.BlockSpec((1, e), lambda b, l: (0, 0)),
        ],
        out_specs=pl.BlockSpec((bb, tl, e), lambda b, l: (b, l, 0)),
        out_shape=jax.ShapeDtypeStruct((n, L, e), F32),
        compiler_params=_params("parallel", "parallel"),
        name="s5_glu",
    )(y, uz, d_skip.reshape(1, e), w_glu_bf16, b_glu.reshape(1, e))


def s5_tables(a_re, a_im, b_re, b_im, c_re, c_im, log_dt):
    g, p = a_re.shape
    ntile = g // S5_TILE_GROUPS
    dt = jnp.exp(log_dt.astype(F32))[:, None]
    lr, li = a_re.astype(F32), a_im.astype(F32)
    mag = jnp.exp(lr * dt)
    lb_re, lb_im = mag * jnp.cos(li * dt), mag * jnp.sin(li * dt)
    nr, ni = lb_re - 1.0, lb_im
    den = lr * lr + li * li
    w_re = (nr * lr + ni * li) / den
    w_im = (ni * lr - nr * li) / den
    br, bi = b_re.astype(F32), b_im.astype(F32)
    bb_re = w_re[..., None] * br - w_im[..., None] * bi
    bb_im = w_re[..., None] * bi + w_im[..., None] * br
    eye = jnp.eye(S5_TILE_GROUPS, dtype=F32)

    def in_layout(bb):
        t4 = bb.reshape(ntile, S5_TILE_GROUPS, p, SSM_GROUP)
        return jnp.einsum('jgpc,gh->jgchp', t4, eye).reshape(ntile, S5_TILE_CH, S5_TILE_STATES)

    def out_layout(cc):
        t4 = cc.reshape(ntile, S5_TILE_GROUPS, SSM_GROUP, p)
        return jnp.einsum('jgcp,gh->jgphc', t4, eye).reshape(ntile, S5_TILE_STATES, S5_TILE_CH)

    bw = jnp.concatenate([in_layout(bb_re), in_layout(bb_im)], axis=2).astype(BF16)
    cw = jnp.concatenate([out_layout(c_re.astype(F32)), -out_layout(c_im.astype(F32))], axis=1).astype(BF16)
    lam = jnp.stack([lb_re.reshape(ntile, SUBLANES, LANES), lb_im.reshape(ntile, SUBLANES, LANES)], axis=1)
    return bw, cw, lam


def _rope_kernel(q_ref, k_ref, v_ref, cos_ref, sa_ref, sb_ref, qo_ref, kf_ref, kb_ref, vo_ref, *,
                 qscale, transpose_v):
    cos = cos_ref[...][None]
    sa = sa_ref[...][None]
    sb = sb_ref[...][None]
    bb, tl, e = q_ref.shape

    def rot(x):
        x2 = x.reshape(bb * tl, LANES)
        up = pltpu.roll(x2, LANES - SUBLANES, 1).reshape(bb, tl, LANES)
        dn = pltpu.roll(x2, SUBLANES, 1).reshape(bb, tl, LANES)
        return x * cos + up * sa + dn * sb

    for tile in range(e // LANES):
        sl = slice(tile * LANES, (tile + 1) * LANES)
        qo_ref[:, :, sl] = (rot(q_ref[:, :, sl]) * qscale).astype(BF16)
        kr = rot(k_ref[:, :, sl])
        kf_ref[:, :, sl] = kr
        kb_ref[:, :, sl] = kr.astype(BF16)
        if transpose_v:
            vo_ref[tile, :DIFF_ROW, :] = v_ref[0, :, sl].T.astype(BF16)
            vo_ref[tile, DIFF_ROW:, :] = jnp.ones((VT_ROWS - DIFF_ROW, tl), BF16)
    if not transpose_v:
        vo_ref[...] = v_ref[...].astype(BF16)


def rope_tables(pos):
    half = ROPE_DIM // 2
    assert half == SUBLANES
    inv = ROPE_THETA ** (-jnp.arange(half, dtype=F32) / half)
    ang = pos.astype(F32)[:, None] * inv[None, :]
    cos, sin = jnp.cos(ang), jnp.sin(ang)
    lane = jnp.arange(LANES) % DIFF_HEAD_DIM
    idx = lane % half
    first = (lane < half)[None, :]
    second = ((lane >= half) & (lane < ROPE_DIM))[None, :]
    cos_t = jnp.where(first | second, cos[:, idx], 1.0)
    sa_t = jnp.where(first, -sin[:, idx], 0.0)
    sb_t = jnp.where(second, sin[:, idx], 0.0)
    return cos_t, sa_t, sb_t


def rope_qkv(qkvz, tables, bb, tl, transpose_v):
    n, L, e4 = qkvz.shape
    e = e4 // 4
    heads = e // DIFF_ROW
    cos_t, sa_t, sb_t = tables
    tab = pl.BlockSpec((tl, LANES), lambda b, l: (l, 0))
    blk = lambda col: pl.BlockSpec((bb, tl, e), lambda b, l: (b, l, col))
    if transpose_v:
        assert bb == 1
        v_spec = pl.BlockSpec((None, heads, VT_ROWS, tl), lambda b, l: (b, 0, 0, l))
        v_shape = jax.ShapeDtypeStruct((n, heads, VT_ROWS, L), BF16)
    else:
        v_spec = blk(0)
        v_shape = jax.ShapeDtypeStruct((n, L, e), BF16)
    return pl.pallas_call(
        functools.partial(_rope_kernel, qscale=LOG2E * DIFF_HEAD_DIM ** -0.5, transpose_v=transpose_v),
        grid=(n // bb, L // tl),
        in_specs=[blk(0), blk(1), blk(2), tab, tab, tab],
        out_specs=[blk(0), blk(0), blk(0), v_spec],
        out_shape=[
            jax.ShapeDtypeStruct((n, L, e), BF16),
            jax.ShapeDtypeStruct((n, L, e), F32),
            jax.ShapeDtypeStruct((n, L, e), BF16),
            v_shape,
        ],
        compiler_params=_params("parallel", "parallel"),
        name="rope_qkv",
    )(qkvz, qkvz, qkvz, cos_t, sa_t, sb_t)


def _flash_kernel(lam_ref, q_ref, k_ref, vt_ref, g_ref, o_ref, m_scr, acc_scr, *, tq, tk, post_scale):
    i = pl.program_id(2)
    q = q_ref[...]
    lane = lax.broadcasted_iota(jnp.int32, q.shape, 1)
    zero = jnp.zeros_like(q)
    qmaps = (jnp.where(lane < DIFF_HEAD_DIM, q, zero), jnp.where(lane >= DIFF_HEAD_DIM, q, zero))
    contract_last = (((1,), (1,)), ((), ()))

    m_scr[...] = jnp.full(m_scr.shape, -jnp.inf, F32)
    acc_scr[...] = jnp.zeros(acc_scr.shape, F32)

    def block(j, mask):
        k0 = pl.multiple_of(j * tk, tk)
        k = k_ref[pl.ds(k0, tk), :]
        vt = vt_ref[:, pl.ds(k0, tk)]
        for mp in range(2):
            s = lax.dot_general(k, qmaps[mp], contract_last, preferred_element_type=F32)
            if mask is not None:
                s = jnp.where(mask, s, -jnp.inf)
            m_prev = m_scr[mp]
            m_new = jnp.maximum(m_prev, jnp.max(s, axis=0, keepdims=True))
            alpha = jnp.exp2(m_prev - m_new)
            p = jnp.exp2(s - m_new).astype(BF16)
            acc_scr[mp] = alpha * acc_scr[mp] + jnp.dot(vt, p, preferred_element_type=F32)
            m_scr[mp] = m_new

    def body(j, carry):
        block(j, None)
        return carry

    sub = tq // tk
    lax.fori_loop(0, i * sub, body, 0)
    keys = lax.broadcasted_iota(jnp.int32, (tk, tq), 0)
    qrys = lax.broadcasted_iota(jnp.int32, (tk, tq), 1)
    for d in range(sub):
        block(i * sub + d, keys + d * tk <= qrys)

    lam = lam_ref[0, 0]
    acc0 = acc_scr[0]
    acc1 = acc_scr[1]
    a = (acc0[:DIFF_ROW] * (1.0 / acc0[DIFF_ROW:DIFF_ROW + 1])
         - lam * (acc1[:DIFF_ROW] * (1.0 / acc1[DIFF_ROW:DIFF_ROW + 1])))
    ms = jnp.mean(a * a, axis=0, keepdims=True)
    a = a * lax.rsqrt(ms + RMS_EPS)
    o_ref[...] = a.T * g_ref[...] * post_scale


def flash_diff_attn(q, k, vt, lam, subln, post_scale, tq, tk):
    n, L, e = q.shape
    heads = e // DIFF_ROW
    return pl.pallas_call(
        functools.partial(_flash_kernel, tq=tq, tk=tk, post_scale=post_scale),
        grid=(n, heads, L // tq),
        in_specs=[
            pl.BlockSpec(memory_space=pltpu.SMEM),
            pl.BlockSpec((None, tq, DIFF_ROW), lambda b, h, i: (b, i, h)),
            pl.BlockSpec((None, L, DIFF_ROW), lambda b, h, i: (b, 0, h)),
            pl.BlockSpec((None, None, VT_ROWS, L), lambda b, h, i: (b, h, 0, 0)),
            pl.BlockSpec((1, DIFF_ROW), lambda b, h, i: (0, 0)),
        ],
        out_specs=pl.BlockSpec((None, tq, DIFF_ROW), lambda b, h, i: (b, i, h)),
        out_shape=jax.ShapeDtypeStruct((n, L, e), F32),
        scratch_shapes=[
            pltpu.VMEM((2, 1, tq), F32),
            pltpu.VMEM((2, VT_ROWS, tq), F32),
        ],
        compiler_params=_params("parallel", "parallel", "arbitrary"),
        name="flash_diff_attn",
    )(lam.reshape(1, 1), q, k, vt, subln.reshape(1, DIFF_ROW))


PAIR_ROWS = 4 * SUBLANES
PAIR_COLS = 2 * DIFF_ROW
PAGES_PER_STEP = 4


def _cached_attn_kernel(pt_ref, lam_ref, qd_ref, kn_ref, vn_ref, *rest, npairs, heads, page, npg, post_scale):
    del pt_ref
    kc_refs = rest[:npg]
    vc_refs = rest[npg:2 * npg]
    g_ref, o_ref, m_scr, l_scr, acc_scr = rest[2 * npg:]
    p = pl.program_id(1)
    nt = SUBLANES
    contract_last = (((1,), (1,)), ((), ()))

    def scores(key_tile):
        parts = [lax.dot_general(qd_ref[hp], key_tile(hp), contract_last, preferred_element_type=F32)
                 for hp in range(npairs)]
        return jnp.concatenate(parts, axis=0)

    def pair_slice(hp):
        return slice(hp * PAIR_ROWS, (hp + 1) * PAIR_ROWS)

    @pl.when(p == 0)
    def _():
        s = scores(lambda hp: kn_ref[:, hp * PAIR_COLS:(hp + 1) * PAIR_COLS])
        qpos = lax.broadcasted_iota(jnp.int32, s.shape, 0) % nt
        kpos = lax.broadcasted_iota(jnp.int32, s.shape, 1)
        s = jnp.where(kpos <= qpos, s, -jnp.inf)
        m = jnp.max(s, axis=1, keepdims=True)
        pr = jnp.exp2(s - m)
        m_scr[...] = m
        l_scr[...] = jnp.sum(pr, axis=1, keepdims=True)
        prb = pr.astype(BF16)
        for hp in range(npairs):
            acc_scr[hp] = jnp.dot(prb[pair_slice(hp)], vn_ref[:, hp * PAIR_COLS:(hp + 1) * PAIR_COLS],
                                  preferred_element_type=F32)

    def head_rows(ref, h):
        return ref[pl.ds(h, page, stride=heads), :]

    def page_tile(refs, hp):
        tiles = [jnp.concatenate([head_rows(ref, 2 * hp), head_rows(ref, 2 * hp + 1)], axis=1) for ref in refs]
        return jnp.concatenate(tiles, axis=0).astype(BF16)

    s = scores(lambda hp: page_tile(kc_refs, hp))
    m_prev = m_scr[...]
    m_new = jnp.maximum(m_prev, jnp.max(s, axis=1, keepdims=True))
    alpha = jnp.exp2(m_prev - m_new)
    pr = jnp.exp2(s - m_new)
    l_scr[...] = alpha * l_scr[...] + jnp.sum(pr, axis=1, keepdims=True)
    m_scr[...] = m_new
    prb = pr.astype(BF16)
    for hp in range(npairs):
        acc_scr[hp] = alpha[pair_slice(hp)] * acc_scr[hp] + jnp.dot(
            prb[pair_slice(hp)], page_tile(vc_refs, hp), preferred_element_type=F32)

    @pl.when(p == pl.num_programs(1) - 1)
    def _():
        lam = lam_ref[0, 0]
        l_all = l_scr[...]
        for hp in range(npairs):
            acc = acc_scr[hp]
            for hh in range(2):
                r0 = hp * PAIR_ROWS + hh * 2 * nt
                a0 = acc[hh * 2 * nt:hh * 2 * nt + nt, hh * DIFF_ROW:(hh + 1) * DIFF_ROW]
                a1 = acc[hh * 2 * nt + nt:(hh + 1) * 2 * nt, hh * DIFF_ROW:(hh + 1) * DIFF_ROW]
                a = a0 / l_all[r0:r0 + nt] - lam * (a1 / l_all[r0 + nt:r0 + 2 * nt])
                ms = jnp.mean(a * a, axis=-1, keepdims=True)
                h = 2 * hp + hh
                o_ref[:, h * DIFF_ROW:(h + 1) * DIFF_ROW] = a * lax.rsqrt(ms + RMS_EPS) * g_ref[...] * post_scale


def cached_diff_attn(q_bf, k_bf, v_bf, cache_k, cache_v, layer_j, page_table, lam, subln, post_scale):
    n, t, e = q_bf.shape
    assert t == SUBLANES
    heads = e // DIFF_ROW
    npairs = heads // 2
    n_pages = page_table.shape[1]
    n_pool, n_layers, page = cache_k.shape[:3]
    npg = math.gcd(PAGES_PER_STEP, n_pages)
    q5 = q_bf.reshape(n, t, npairs, 2, 2, DIFF_HEAD_DIM)
    eye2 = jnp.eye(2, dtype=BF16)
    qd = jnp.einsum('ntphjd,hg,jk->nphjtgkd', q5, eye2, eye2).reshape(n, npairs, PAIR_ROWS, PAIR_COLS)
    pad = ((0, 0), (0, BF16_SUBLANES - t), (0, 0))
    kn = jnp.pad(k_bf, pad)
    vn = jnp.pad(v_bf, pad)
    ck = cache_k.reshape(n_pool, n_layers, page * heads, DIFF_ROW)
    cv = cache_v.reshape(n_pool, n_layers, page * heads, DIFF_ROW)

    def cache_spec(i):
        return pl.BlockSpec((None, None, page * heads, DIFF_ROW),
                            lambda b, p, pt: (pt[b, npg * p + i], layer_j, 0, 0))

    new_spec = pl.BlockSpec((None, BF16_SUBLANES, e), lambda b, p, pt: (b, 0, 0))
    grid_spec = pltpu.PrefetchScalarGridSpec(
        num_scalar_prefetch=1,
        grid=(n, n_pages // npg),
        in_specs=[
            pl.BlockSpec(memory_space=pltpu.SMEM),
            pl.BlockSpec((None, npairs, PAIR_ROWS, PAIR_COLS), lambda b, p, pt: (b, 0, 0, 0)),
            new_spec, new_spec,
            *[cache_spec(i) for i in range(npg)],
            *[cache_spec(i) for i in range(npg)],
            pl.BlockSpec((1, DIFF_ROW), lambda b, p, pt: (0, 0)),
        ],
        out_specs=pl.BlockSpec((None, t, e), lambda b, p, pt: (b, 0, 0)),
        scratch_shapes=[
            pltpu.VMEM((npairs * PAIR_ROWS, 1), F32),
            pltpu.VMEM((npairs * PAIR_ROWS, 1), F32),
            pltpu.VMEM((npairs, PAIR_ROWS, PAIR_COLS), F32),
        ],
    )
    return pl.pallas_call(
        functools.partial(_cached_attn_kernel, npairs=npairs, heads=heads, page=page, npg=npg,
                          post_scale=post_scale),
        grid_spec=grid_spec,
        out_shape=jax.ShapeDtypeStruct((n, t, e), F32),
        compiler_params=_params("parallel", "arbitrary"),
        name="cached_diff_attn",
    )(page_table, lam.reshape(1, 1), qd, kn, vn, *([ck] * npg), *([cv] * npg), subln.reshape(1, DIFF_ROW))


def _hgrn_head(q, fz, v, lb, g, st, tc):
    dk = q.shape[1]
    logf = jnp.log(lb + (1.0 - lb) * jax.nn.sigmoid(fz))
    kk = (1.0 - lb) * jax.nn.sigmoid(-fz)
    row = lax.broadcasted_iota(jnp.int32, (tc, dk), 0)
    b = logf
    k = 1
    while k < tc:
        b = b + jnp.where(row >= k, pltpu.roll(b, k, 0), 0.0)
        k *= 2
    contract_last = (((1,), (1,)), ((), ()))
    contract_first = (((0,), (0,)), ((), ()))
    o = lax.dot_general((q * jnp.exp(b)).astype(BF16), st.astype(BF16), contract_last,
                        preferred_element_type=F32)
    rin = row % SUBLANES
    for d in range(SUBLANES):
        if d == 0:
            w = q * kk
            vr = v
        else:
            w = q * pltpu.roll(kk, d, 0) * jnp.exp(b - pltpu.roll(b, d, 0))
            vr = pltpu.roll(v, d, 0)
        cd = jnp.sum(jnp.where(rin >= d, w, 0.0), axis=1, keepdims=True)
        o = o + cd * vr
    if tc > SUBLANES:
        trow = lax.broadcasted_iota(jnp.int32, (tc, tc), 0)
        tcol = lax.broadcasted_iota(jnp.int32, (tc, tc), 1)
        att = jnp.zeros((tc, tc), F32)
        blk = 2 * SUBLANES
        while blk <= tc:
            half = blk // 2
            b3 = b.reshape(tc // blk, blk, dk)
            anchor = jnp.broadcast_to(b3[:, half - 1:half, :], b3.shape).reshape(tc, dk)
            pos = row % blk
            qt = jnp.where(pos >= half, q * jnp.exp(b - anchor), 0.0)
            kt = jnp.where(pos < half, kk * jnp.exp(anchor - b), 0.0)
            a = lax.dot_general(qt.astype(BF16), kt.astype(BF16), contract_last, preferred_element_type=F32)
            att = att + jnp.where((trow // blk) == (tcol // blk), a, 0.0)
            blk *= 2
        o = o + jnp.dot(att.astype(BF16), v.astype(BF16), preferred_element_type=F32)
    ms = jnp.mean(o * o, axis=-1, keepdims=True)
    o = o * lax.rsqrt(ms + RMS_EPS) * g
    b_end = b[tc - 1:, :]
    khat = kk * jnp.exp(b_end - b)
    st_new = st * jnp.exp(b_end) + lax.dot_general(v.astype(BF16), khat.astype(BF16), contract_first,
                                                   preferred_element_type=F32)
    return o, st_new


def _hgrn_kernel(q_ref, fz_ref, v_ref, lb_ref, g_ref, s0_ref, o_ref, sout_ref, st_scr, *, tc, hp):
    c = pl.program_id(2)

    @pl.when(c == 0)
    def _():
        for h in range(hp):
            st_scr[h] = s0_ref[h].T

    for h in range(hp):
        hs = slice(h * LANES, (h + 1) * LANES)
        o, st_new = _hgrn_head(q_ref[:, hs], fz_ref[:, hs], v_ref[:, hs], lb_ref[:, hs], g_ref[:, hs],
                               st_scr[h], tc)
        o_ref[:, hs] = o
        st_scr[h] = st_new

    @pl.when(c == pl.num_programs(2) - 1)
    def _():
        for h in range(hp):
            sout_ref[h] = st_scr[h].T


def hgrn_scan(qfvz, lb, g_norm, s0, tc, hp):
    n, L, e4 = qfvz.shape
    e = e4 // 4
    heads = s0.shape[1]
    dk, dv = s0.shape[2], s0.shape[3]
    assert dk == LANES and dv == LANES
    ng = heads // hp
    col = lambda base: pl.BlockSpec((None, tc, hp * dk), lambda b, h, c: (b, c, base * ng + h))
    vec = pl.BlockSpec((1, hp * dk), lambda b, h, c: (0, h))
    st = pl.BlockSpec((None, hp, dk, dv), lambda b, h, c: (b, h, 0, 0))
    return pl.pallas_call(
        functools.partial(_hgrn_kernel, tc=tc, hp=hp),
        grid=(n, ng, L // tc),
        in_specs=[col(0), col(1), col(2), vec, vec, st],
        out_specs=[pl.BlockSpec((None, tc, hp * dv), lambda b, h, c: (b, c, h)), st],
        out_shape=[jax.ShapeDtypeStruct((n, L, e), F32), jax.ShapeDtypeStruct(s0.shape, F32)],
        scratch_shapes=[pltpu.VMEM((hp, dv, dk), F32)],
        compiler_params=_params("parallel", "parallel", "arbitrary"),
        name="hgrn_scan",
    )(qfvz, qfvz, qfvz, lb.reshape(1, e), g_norm.reshape(1, e), s0)


def kernel(x_prompt, x_sample, cache_attn_k, cache_attn_v, state_ssm_re, state_ssm_im, state_hgrn, page_table, c_prompt, c_sample, ada_w, ada_b, ln_g, ln_b, ssm_w_in, ssm_a_re, ssm_a_im, ssm_b_re, ssm_b_im, ssm_c_re, ssm_c_im, ssm_d, ssm_log_dt, ssm_w_glu, ssm_b_glu, ssm_w_out, attn_w_in, attn_lam_q1, attn_lam_k1, attn_lam_q2, attn_lam_k2, attn_subln, attn_w_out, hgrn_w_in, hgrn_lb_logits, hgrn_g_norm, hgrn_w_out):
    depth = ada_w.shape[0]
    alpha = (2 * depth) ** 0.25
    past_len = page_table.shape[1] * cache_attn_k.shape[2]
    ssm_layers = [i for i in range(depth) if i % N_MIXERS == 0]
    attn_layers = [i for i in range(depth) if i % N_MIXERS == 1]
    hgrn_layers = [i for i in range(depth) if i % N_MIXERS == 2]

    n_p, n_s = x_prompt.shape[0], x_sample.shape[0]
    rows = n_p + n_s
    rows_pad = -(-rows // SUBLANES) * SUBLANES
    c_all = jnp.pad(jnp.concatenate([c_prompt, c_sample], axis=0), ((0, rows_pad - rows), (0, 0)))
    mod_all = ada_modulation(c_all, ada_w, ada_b)

    lb_cum = jnp.cumsum(jax.nn.softmax(hgrn_lb_logits.astype(F32), axis=0), axis=0)
    lb_all = lb_cum - lb_cum[0]

    s5_tabs = [s5_tables(ssm_a_re[j], ssm_a_im[j], ssm_b_re[j], ssm_b_im[j], ssm_c_re[j], ssm_c_im[j],
                         ssm_log_dt[j]) for j in range(len(ssm_layers))]
    ntile = ssm_a_re.shape[1] // S5_TILE_GROUPS
    hgrn_heads = state_hgrn.shape[2]

    def run_group(x, mod_rows, pos, sample):
        n, L, _ = x.shape
        if sample:
            blk_in = dict(bb=min(n, 128), tl=L)
            blk_out = dict(bb=min(n, 32), tl=L)
            blk_glu = dict(bb=min(n, 32), tl=L)
            blk_rope = dict(bb=min(n, 32), tl=L)
            s5_blk = dict(bs=min(n, 16), t=L)
            hgrn_blk = dict(tc=L, hp=hgrn_heads)
        else:
            blk_in = dict(bb=1, tl=min(L, 1024))
            blk_out = dict(bb=1, tl=min(L, 256))
            blk_glu = dict(bb=1, tl=min(L, 256))
            blk_rope = dict(bb=1, tl=min(L, 256))
            s5_blk = dict(bs=min(n, 4), t=min(L, 256))
            hgrn_blk = dict(tc=min(L, 128), hp=hgrn_heads // 2)
        rope_tabs = rope_tables(pos)
        ssm_re, ssm_im, kv_k, kv_v, hgrn_s = [], [], [], [], []
        for i in range(depth):
            mod = mod_all[i, mod_rows][:, None, :]
            if i % N_MIXERS == 0:
                j = ssm_layers.index(i)
                uz = mod_inproj(x, mod, ssm_w_in[j].astype(BF16), **blk_in)
                bw, cw, lam_bar = s5_tabs[j]
                if sample:
                    h0r = state_ssm_re[:, j].astype(F32).reshape(n, ntile, SUBLANES, LANES)
                    h0i = state_ssm_im[:, j].astype(F32).reshape(n, ntile, SUBLANES, LANES)
                else:
                    h0r = jnp.zeros((n, ntile, SUBLANES, LANES), F32)
                    h0i = h0r
                y, hr, hi = s5_scan(uz, bw, cw, lam_bar, h0r, h0i, **s5_blk)
                a = s5_glu(y, uz, ssm_d[j], ssm_w_glu[j].astype(BF16), ssm_b_glu[j], **blk_glu)
                x = out_ln(a, uz, 1, x, mod, ssm_w_out[j].astype(BF16), ln_g[i], ln_b[i], alpha, **blk_out)
                ssm_re.append(hr.reshape(n, -1, SSM_STATE))
                ssm_im.append(hi.reshape(n, -1, SSM_STATE))
            elif i % N_MIXERS == 1:
                j = attn_layers.index(i)
                lam_init = 0.8 - 0.6 * math.exp(-0.3 * i)
                lam = (jnp.exp(jnp.sum(attn_lam_q1[j].astype(F32) * attn_lam_k1[j].astype(F32)))
                       - jnp.exp(jnp.sum(attn_lam_q2[j].astype(F32) * attn_lam_k2[j].astype(F32)))
                       + lam_init)
                qkvz = mod_inproj(x, mod, attn_w_in[j].astype(BF16), **blk_in)
                e = qkvz.shape[-1] // 4
                heads = e // DIFF_ROW
                q_bf, k_f, k_bf, v_op = rope_qkv(qkvz, rope_tabs, transpose_v=not sample, **blk_rope)
                if sample:
                    a = cached_diff_attn(q_bf, k_bf, v_op, cache_attn_k, cache_attn_v, j, page_table, lam,
                                         attn_subln[j], 1.0 - lam_init)
                else:
                    tq = min(L, 1024)
                    a = flash_diff_attn(q_bf, k_bf, v_op, lam, attn_subln[j], 1.0 - lam_init, tq=tq, tk=tq)
                x = out_ln(a, qkvz, 3, x, mod, attn_w_out[j].astype(BF16), ln_g[i], ln_b[i], alpha, **blk_out)
                kv_k.append(k_f.reshape(n, L, heads, DIFF_ROW))
                kv_v.append(qkvz[:, :, 2 * e:3 * e].reshape(n, L, heads, DIFF_ROW))
            else:
                j = hgrn_layers.index(i)
                qfvz = mod_inproj(x, mod, hgrn_w_in[j].astype(BF16), **blk_in)
                if sample:
                    s0 = state_hgrn[:, j].astype(F32)
                else:
                    s0 = jnp.zeros((n,) + state_hgrn.shape[2:], F32)
                o, s_fin = hgrn_scan(qfvz, lb_all[i], hgrn_g_norm[j].astype(F32), s0, **hgrn_blk)
                x = out_ln(o, qfvz, 3, x, mod, hgrn_w_out[j].astype(BF16), ln_g[i], ln_b[i], alpha, **blk_out)
                hgrn_s.append(s_fin)
        return (x, jnp.stack(kv_k, axis=2), jnp.stack(kv_v, axis=2),
                jnp.stack(ssm_re, axis=1), jnp.stack(ssm_im, axis=1), jnp.stack(hgrn_s, axis=1))

    pos_p = jnp.arange(x_prompt.shape[1], dtype=jnp.int32)
    pos_s = past_len + jnp.arange(x_sample.shape[1], dtype=jnp.int32)
    y_p, k_p, v_p, sr_p, si_p, hs_p = run_group(x_prompt, slice(0, n_p), pos_p, False)
    y_s, k_s, v_s, sr_s, si_s, hs_s = run_group(x_sample, slice(n_p, n_p + n_s), pos_s, True)
    return (y_p, y_s, k_p, v_p, sr_p, si_p, hs_p, k_s, v_s, sr_s, si_s, hs_s)
```

```python
import functools
import math

import jax
import jax.numpy as jnp
from jax import lax
from jax.experimental import pallas as pl
from jax.experimental.pallas import tpu as pltpu

F32 = jnp.float32
BF16 = jnp.bfloat16

LANES = 128
SUBLANES = 8
BF16_SUBLANES = 16
VMEM_LIMIT_BYTES = 56 * 1024 * 1024

N_MIXERS = 3
SSM_GROUP = 16
SSM_STATE = 64
DIFF_HEAD_DIM = 64
DIFF_ROW = 2 * DIFF_HEAD_DIM
ROPE_DIM = DIFF_HEAD_DIM // 4
ROPE_THETA = 500000.0
LN_EPS = 1e-5
RMS_EPS = 1e-6
LOG2E = 1.4426950408889634

S5_TILE_GROUPS = 16
S5_TILE_CH = S5_TILE_GROUPS * SSM_GROUP
S5_TILE_STATES = S5_TILE_GROUPS * SSM_STATE
S5_CHUNKS = S5_TILE_STATES // LANES
assert S5_CHUNKS == SUBLANES

VT_ROWS = DIFF_ROW + BF16_SUBLANES


def _params(*sem):
    return pltpu.CompilerParams(dimension_semantics=sem, vmem_limit_bytes=VMEM_LIMIT_BYTES)


def _silu(x):
    return x * jax.nn.sigmoid(x)


def _ada_kernel(c_ref, w_ref, b_ref, o_ref):
    s = _silu(c_ref[...]).astype(BF16)
    o_ref[...] = jnp.dot(s, w_ref[...].astype(BF16), preferred_element_type=F32) + b_ref[...]


def ada_modulation(c_all, ada_w, ada_b):
    depth, d, d3 = ada_w.shape
    rows = c_all.shape[0]
    tn = 1024
    return pl.pallas_call(
        _ada_kernel,
        grid=(depth, d3 // tn),
        in_specs=[
            pl.BlockSpec((rows, d), lambda i, j: (0, 0)),
            pl.BlockSpec((None, d, tn), lambda i, j: (i, 0, j)),
            pl.BlockSpec((None, 1, tn), lambda i, j: (i, 0, j)),
        ],
        out_specs=pl.BlockSpec((None, rows, tn), lambda i, j: (i, 0, j)),
        out_shape=jax.ShapeDtypeStruct((depth, rows, d3), F32),
        compiler_params=_params("parallel", "parallel"),
        name="ada_modulation",
    )(c_all, ada_w, ada_b.reshape(depth, 1, d3))


def _inproj_kernel(x_ref, shift_ref, scale_ref, w_ref, o_ref):
    h = x_ref[...] * (1.0 + scale_ref[...]) + shift_ref[...]
    bb, tl, d = h.shape
    o = jnp.dot(h.reshape(bb * tl, d).astype(BF16), w_ref[...], preferred_element_type=F32)
    o_ref[...] = o.reshape(bb, tl, o.shape[-1])


def mod_inproj(x, mod, w_bf16, bb, tl, tn=1024):
    n, L, d = x.shape
    nout = w_bf16.shape[1]
    return pl.pallas_call(
        _inproj_kernel,
        grid=(n // bb, L // tl, nout // tn),
        in_specs=[
            pl.BlockSpec((bb, tl, d), lambda b, l, j: (b, l, 0)),
            pl.BlockSpec((bb, 1, d), lambda b, l, j: (b, 0, 0)),
            pl.BlockSpec((bb, 1, d), lambda b, l, j: (b, 0, 1)),
            pl.BlockSpec((d, tn), lambda b, l, j: (0, j)),
        ],
        out_specs=pl.BlockSpec((bb, tl, tn), lambda b, l, j: (b, l, j)),
        out_shape=jax.ShapeDtypeStruct((n, L, nout), F32),
        compiler_params=_params("parallel", "parallel", "arbitrary"),
        name="mod_inproj",
    )(x, mod, mod, w_bf16)


def _out_ln_kernel(a_ref, z_ref, x_ref, gate_ref, w_ref, g_ref, b_ref, o_ref, *, alpha):
    act = a_ref[...] * _silu(z_ref[...])
    bb, tl, e = act.shape
    out = jnp.dot(act.reshape(bb * tl, e).astype(BF16), w_ref[...], preferred_element_type=F32)
    out = out.reshape(bb, tl, out.shape[-1])
    r = alpha * x_ref[...] + (1.0 + gate_ref[...]) * out
    mu = jnp.mean(r, axis=-1, keepdims=True)
    dlt = r - mu
    var = jnp.mean(dlt * dlt, axis=-1, keepdims=True)
    o_ref[...] = dlt * lax.rsqrt(var + LN_EPS) * g_ref[...] + b_ref[...]


def out_ln(a, zsrc, zblk, x, mod, w_bf16, ln_g, ln_b, alpha, bb, tl):
    n, L, d = x.shape
    e = a.shape[-1]
    return pl.pallas_call(
        functools.partial(_out_ln_kernel, alpha=alpha),
        grid=(n // bb, L // tl),
        in_specs=[
            pl.BlockSpec((bb, tl, e), lambda b, l: (b, l, 0)),
            pl.BlockSpec((bb, tl, e), lambda b, l: (b, l, zblk)),
            pl.BlockSpec((bb, tl, d), lambda b, l: (b, l, 0)),
            pl.BlockSpec((bb, 1, d), lambda b, l: (b, 0, 2)),
            pl.BlockSpec((e, d), lambda b, l: (0, 0)),
            pl.BlockSpec((1, d), lambda b, l: (0, 0)),
            pl.BlockSpec((1, d), lambda b, l: (0, 0)),
        ],
        out_specs=pl.BlockSpec((bb, tl, d), lambda b, l: (b, l, 0)),
        out_shape=jax.ShapeDtypeStruct((n, L, d), F32),
        compiler_params=_params("parallel", "parallel"),
        name="out_ln",
    )(a, zsrc, x, mod, w_bf16, ln_g.reshape(1, d), ln_b.reshape(1, d))


def _s5_scan_kernel(u_ref, bw_ref, cw_ref, lam_ref, h0r_ref, h0i_ref, y_ref, hr_ref, hi_ref,
                    xr_scr, xi_scr, car_scr, *, bs, t):
    c = pl.program_id(2)
    nst = S5_TILE_STATES
    u = u_ref[...].reshape(bs * t, S5_TILE_CH).astype(BF16)
    bu = jnp.dot(u, bw_ref[...], preferred_element_type=F32)
    for s in range(bs):
        rows = slice(s * t, (s + 1) * t)
        for a in range(S5_CHUNKS):
            xr_scr[s, pl.ds(a, t, stride=SUBLANES), :] = bu[rows, a * LANES:(a + 1) * LANES]
            xi_scr[s, pl.ds(a, t, stride=SUBLANES), :] = bu[rows, nst + a * LANES:nst + (a + 1) * LANES]

    @pl.when(c == 0)
    def _():
        car_scr[0] = h0r_ref[...]
        car_scr[1] = h0i_ref[...]

    lr = lam_ref[0]
    li = lam_ref[1]

    def body(p, carry):
        r0 = pl.multiple_of(p * SUBLANES, SUBLANES)
        out = []
        for s in range(bs):
            hr, hi = carry[2 * s], carry[2 * s + 1]
            nr = lr * hr - li * hi + xr_scr[s, pl.ds(r0, SUBLANES), :]
            ni = lr * hi + li * hr + xi_scr[s, pl.ds(r0, SUBLANES), :]
            xr_scr[s, pl.ds(r0, SUBLANES), :] = nr
            xi_scr[s, pl.ds(r0, SUBLANES), :] = ni
            out += [nr, ni]
        return tuple(out)

    init = []
    for s in range(bs):
        init += [car_scr[0, s], car_scr[1, s]]
    fin = lax.fori_loop(0, t, body, tuple(init), unroll=SUBLANES)
    for s in range(bs):
        car_scr[0, s] = fin[2 * s]
        car_scr[1, s] = fin[2 * s + 1]

    for s in range(bs):
        parts = [xr_scr[s, pl.ds(a, t, stride=SUBLANES), :] for a in range(S5_CHUNKS)]
        parts += [xi_scr[s, pl.ds(a, t, stride=SUBLANES), :] for a in range(S5_CHUNKS)]
        hcat = jnp.concatenate(parts, axis=1).astype(BF16)
        y_ref[s] = jnp.dot(hcat, cw_ref[...], preferred_element_type=F32)

    @pl.when(c == pl.num_programs(2) - 1)
    def _():
        hr_ref[...] = car_scr[0]
        hi_ref[...] = car_scr[1]


def s5_scan(uz, bw, cw, lam, h0r, h0i, bs, t):
    n, L, _ = uz.shape
    ntile = bw.shape[0]
    nst = S5_TILE_STATES
    st_spec = pl.BlockSpec((bs, None, SUBLANES, LANES), lambda b, j, c: (b, j, 0, 0))
    return pl.pallas_call(
        functools.partial(_s5_scan_kernel, bs=bs, t=t),
        grid=(n // bs, ntile, L // t),
        in_specs=[
            pl.BlockSpec((bs, t, S5_TILE_CH), lambda b, j, c: (b, c, j)),
            pl.BlockSpec((None, S5_TILE_CH, 2 * nst), lambda b, j, c: (j, 0, 0)),
            pl.BlockSpec((None, 2 * nst, S5_TILE_CH), lambda b, j, c: (j, 0, 0)),
            pl.BlockSpec((None, 2, SUBLANES, LANES), lambda b, j, c: (j, 0, 0, 0)),
            st_spec, st_spec,
        ],
        out_specs=[
            pl.BlockSpec((bs, t, S5_TILE_CH), lambda b, j, c: (b, c, j)),
            st_spec, st_spec,
        ],
        out_shape=[
            jax.ShapeDtypeStruct((n, L, ntile * S5_TILE_CH), F32),
            jax.ShapeDtypeStruct((n, ntile, SUBLANES, LANES), F32),
            jax.ShapeDtypeStruct((n, ntile, SUBLANES, LANES), F32),
        ],
        scratch_shapes=[
            pltpu.VMEM((bs, t * SUBLANES, LANES), F32),
            pltpu.VMEM((bs, t * SUBLANES, LANES), F32),
            pltpu.VMEM((2, bs, SUBLANES, LANES), F32),
        ],
        compiler_params=_params("parallel", "parallel", "arbitrary"),
        name="s5_scan",
    )(uz, bw, cw, lam, h0r, h0i)


def _s5_out_kernel(y_ref, u_ref, z_ref, d_ref, wg_ref, bg_ref, x_ref, gate_ref, w_ref, g_ref, b_ref, o_ref, *,
                   alpha):
    y = jax.nn.gelu(y_ref[...] + d_ref[...] * u_ref[...])
    bb, tl, e = y.shape
    y2 = y.reshape(bb * tl, e)
    glu = jnp.dot(y2.astype(BF16), wg_ref[...], preferred_element_type=F32) + bg_ref[...]
    act = y2 * jax.nn.sigmoid(glu) * _silu(z_ref[...].reshape(bb * tl, e))
    out = jnp.dot(act.astype(BF16), w_ref[...], preferred_element_type=F32)
    out = out.reshape(bb, tl, out.shape[-1])
    r = alpha * x_ref[...] + (1.0 + gate_ref[...]) * out
    mu = jnp.mean(r, axis=-1, keepdims=True)
    dlt = r - mu
    var = jnp.mean(dlt * dlt, axis=-1, keepdims=True)
    o_ref[...] = dlt * lax.rsqrt(var + LN_EPS) * g_ref[...] + b_ref[...]


def s5_out_ln(y, uz, d_skip, w_glu_bf16, b_glu, x, mod, w_bf16, ln_g, ln_b, alpha, bb, tl):
    n, L, d = x.shape
    e = y.shape[-1]
    return pl.pallas_call(
        functools.partial(_s5_out_kernel, alpha=alpha),
        grid=(n // bb, L // tl),
        in_specs=[
            pl.BlockSpec((bb, tl, e), lambda b, l: (b, l, 0)),
            pl.BlockSpec((bb, tl, e), lambda b, l: (b, l, 0)),
            pl.BlockSpec((bb, tl, e), lambda b, l: (b, l, 1)),
            pl.BlockSpec((1, e), lambda b, l: (0, 0)),
            pl.BlockSpec((e, e), lambda b, l: (0, 0)),
            pl.BlockSpec((1, e), lambda b, l: (0, 0)),
            pl.BlockSpec((bb, tl, d), lambda b, l: (b, l, 0)),
            pl.BlockSpec((bb, 1, d), lambda b, l: (b, 0, 2)),
            pl.BlockSpec((e, d), lambda b, l: (0, 0)),
            pl.BlockSpec((1, d), lambda b, l: (0, 0)),
            pl.BlockSpec((1, d), lambda b, l: (0, 0)),
        ],
        out_specs=pl.BlockSpec((bb, tl, d), lambda b, l: (b, l, 0)),
        out_shape=jax.ShapeDtypeStruct((n, L, d), F32),
        compiler_params=_params("parallel", "parallel"),
        name="s5_out_ln",
    )(y, uz, uz, d_skip.reshape(1, e), w_glu_bf16, b_glu.reshape(1, e), x, mod, w_bf16,
      ln_g.reshape(1, d), ln_b.reshape(1, d))


def s5_tables(a_re, a_im, b_re, b_im, c_re, c_im, log_dt):
    g, p = a_re.shape
    ntile = g // S5_TILE_GROUPS
    dt = jnp.exp(log_dt.astype(F32))[:, None]
    lr, li = a_re.astype(F32), a_im.astype(F32)
    mag = jnp.exp(lr * dt)
    lb_re, lb_im = mag * jnp.cos(li * dt), mag * jnp.sin(li * dt)
    nr, ni = lb_re - 1.0, lb_im
    den = lr * lr + li * li
    w_re = (nr * lr + ni * li) / den
    w_im = (ni * lr - nr * li) / den
    br, bi = b_re.astype(F32), b_im.astype(F32)
    bb_re = w_re[..., None] * br - w_im[..., None] * bi
    bb_im = w_re[..., None] * bi + w_im[..., None] * br
    eye = jnp.eye(S5_TILE_GROUPS, dtype=F32)

    def in_layout(bb):
        t4 = bb.reshape(ntile, S5_TILE_GROUPS, p, SSM_GROUP)
        return jnp.einsum('jgpc,gh->jgchp', t4, eye).reshape(ntile, S5_TILE_CH, S5_TILE_STATES)

    def out_layout(cc):
        t4 = cc.reshape(ntile, S5_TILE_GROUPS, SSM_GROUP, p)
        return jnp.einsum('jgcp,gh->jgphc', t4, eye).reshape(ntile, S5_TILE_STATES, S5_TILE_CH)

    bw = jnp.concatenate([in_layout(bb_re), in_layout(bb_im)], axis=2).astype(BF16)
    cw = jnp.concatenate([out_layout(c_re.astype(F32)), -out_layout(c_im.astype(F32))], axis=1).astype(BF16)
    lam = jnp.stack([lb_re.reshape(ntile, SUBLANES, LANES), lb_im.reshape(ntile, SUBLANES, LANES)], axis=1)
    return bw, cw, lam


def _rope_kernel(q_ref, k_ref, v_ref, cos_ref, sa_ref, sb_ref, qo_ref, kf_ref, kb_ref, vf_ref, vo_ref, *,
                 qscale, transpose_v):
    cos = cos_ref[...][None]
    sa = sa_ref[...][None]
    sb = sb_ref[...][None]
    bb, tl, e = q_ref.shape

    def rot(x):
        x2 = x.reshape(bb * tl, LANES)
        up = pltpu.roll(x2, LANES - SUBLANES, 1).reshape(bb, tl, LANES)
        dn = pltpu.roll(x2, SUBLANES, 1).reshape(bb, tl, LANES)
        return x * cos + up * sa + dn * sb

    for tile in range(e // LANES):
        sl = slice(tile * LANES, (tile + 1) * LANES)
        qo_ref[:, :, sl] = (rot(q_ref[:, :, sl]) * qscale).astype(BF16)
        kr = rot(k_ref[:, :, sl])
        kf_ref[:, :, sl] = kr
        kb_ref[:, :, sl] = kr.astype(BF16)
        if transpose_v:
            vo_ref[tile, :DIFF_ROW, :] = v_ref[0, :, sl].T.astype(BF16)
            vo_ref[tile, DIFF_ROW:, :] = jnp.ones((VT_ROWS - DIFF_ROW, tl), BF16)
    if not transpose_v:
        vo_ref[...] = v_ref[...].astype(BF16)
    vf_ref[...] = v_ref[...]


def rope_tables(pos):
    half = ROPE_DIM // 2
    assert half == SUBLANES
    inv = ROPE_THETA ** (-jnp.arange(half, dtype=F32) / half)
    ang = pos.astype(F32)[:, None] * inv[None, :]
    cos, sin = jnp.cos(ang), jnp.sin(ang)
    lane = jnp.arange(LANES) % DIFF_HEAD_DIM
    idx = lane % half
    first = (lane < half)[None, :]
    second = ((lane >= half) & (lane < ROPE_DIM))[None, :]
    cos_t = jnp.where(first | second, cos[:, idx], 1.0)
    sa_t = jnp.where(first, -sin[:, idx], 0.0)
    sb_t = jnp.where(second, sin[:, idx], 0.0)
    return cos_t, sa_t, sb_t


def rope_qkv(qkvz, tables, bb, tl, transpose_v):
    n, L, e4 = qkvz.shape
    e = e4 // 4
    heads = e // DIFF_ROW
    cos_t, sa_t, sb_t = tables
    tab = pl.BlockSpec((tl, LANES), lambda b, l: (l, 0))
    blk = lambda col: pl.BlockSpec((bb, tl, e), lambda b, l: (b, l, col))
    if transpose_v:
        assert bb == 1
        v_spec = pl.BlockSpec((None, heads, VT_ROWS, tl), lambda b, l: (b, 0, 0, l))
        v_shape = jax.ShapeDtypeStruct((n, heads, VT_ROWS, L), BF16)
    else:
        v_spec = blk(0)
        v_shape = jax.ShapeDtypeStruct((n, L, e), BF16)
    return pl.pallas_call(
        functools.partial(_rope_kernel, qscale=LOG2E * DIFF_HEAD_DIM ** -0.5, transpose_v=transpose_v),
        grid=(n // bb, L // tl),
        in_specs=[blk(0), blk(1), blk(2), tab, tab, tab],
        out_specs=[blk(0), blk(0), blk(0), blk(0), v_spec],
        out_shape=[
            jax.ShapeDtypeStruct((n, L, e), BF16),
            jax.ShapeDtypeStruct((n, L, e), F32),
            jax.ShapeDtypeStruct((n, L, e), BF16),
            jax.ShapeDtypeStruct((n, L, e), F32),
            v_shape,
        ],
        compiler_params=_params("parallel", "parallel"),
        name="rope_qkv",
    )(qkvz, qkvz, qkvz, cos_t, sa_t, sb_t)


def _flash_kernel(lam_ref, q_ref, k_ref, vt_ref, g_ref, o_ref, m_scr, acc_scr, *, tq, tk, post_scale):
    i = pl.program_id(2)
    q = q_ref[...]
    lane = lax.broadcasted_iota(jnp.int32, q.shape, 1)
    zero = jnp.zeros_like(q)
    qmaps = (jnp.where(lane < DIFF_HEAD_DIM, q, zero), jnp.where(lane >= DIFF_HEAD_DIM, q, zero))
    contract_last = (((1,), (1,)), ((), ()))

    m_scr[...] = jnp.full(m_scr.shape, -jnp.inf, F32)
    acc_scr[...] = jnp.zeros(acc_scr.shape, F32)

    def block(j, mask):
        k0 = pl.multiple_of(j * tk, tk)
        k = k_ref[pl.ds(k0, tk), :]
        vt = vt_ref[:, pl.ds(k0, tk)]
        for mp in range(2):
            s = lax.dot_general(k, qmaps[mp], contract_last, preferred_element_type=F32)
            if mask is not None:
                s = jnp.where(mask, s, -jnp.inf)
            m_prev = m_scr[mp]
            m_new = jnp.maximum(m_prev, jnp.max(s, axis=0, keepdims=True))
            alpha = jnp.exp2(m_prev - m_new)
            p = jnp.exp2(s - m_new).astype(BF16)
            acc_scr[mp] = alpha * acc_scr[mp] + jnp.dot(vt, p, preferred_element_type=F32)
            m_scr[mp] = m_new

    def body(j, carry):
        block(j, None)
        return carry

    sub = tq // tk
    lax.fori_loop(0, i * sub, body, 0)
    keys = lax.broadcasted_iota(jnp.int32, (tk, tq), 0)
    qrys = lax.broadcasted_iota(jnp.int32, (tk, tq), 1)
    for d in range(sub):
        block(i * sub + d, keys + d * tk <= qrys)

    lam = lam_ref[0, 0]
    acc0 = acc_scr[0]
    acc1 = acc_scr[1]
    a = (acc0[:DIFF_ROW] * (1.0 / acc0[DIFF_ROW:DIFF_ROW + 1])
         - lam * (acc1[:DIFF_ROW] * (1.0 / acc1[DIFF_ROW:DIFF_ROW + 1])))
    ms = jnp.mean(a * a, axis=0, keepdims=True)
    a = a * lax.rsqrt(ms + RMS_EPS)
    o_ref[...] = a.T * g_ref[...] * post_scale


def flash_diff_attn(q, k, vt, lam, subln, post_scale, tq, tk):
    n, L, e = q.shape
    heads = e // DIFF_ROW
    return pl.pallas_call(
        functools.partial(_flash_kernel, tq=tq, tk=tk, post_scale=post_scale),
        grid=(n, heads, L // tq),
        in_specs=[
            pl.BlockSpec(memory_space=pltpu.SMEM),
            pl.BlockSpec((None, tq, DIFF_ROW), lambda b, h, i: (b, i, h)),
            pl.BlockSpec((None, L, DIFF_ROW), lambda b, h, i: (b, 0, h)),
            pl.BlockSpec((None, None, VT_ROWS, L), lambda b, h, i: (b, h, 0, 0)),
            pl.BlockSpec((1, DIFF_ROW), lambda b, h, i: (0, 0)),
        ],
        out_specs=pl.BlockSpec((None, tq, DIFF_ROW), lambda b, h, i: (b, i, h)),
        out_shape=jax.ShapeDtypeStruct((n, L, e), F32),
        scratch_shapes=[
            pltpu.VMEM((2, 1, tq), F32),
            pltpu.VMEM((2, VT_ROWS, tq), F32),
        ],
        compiler_params=_params("parallel", "parallel", "arbitrary"),
        name="flash_diff_attn",
    )(lam.reshape(1, 1), q, k, vt, subln.reshape(1, DIFF_ROW))


PAIR_ROWS = 4 * SUBLANES
PAIR_COLS = 2 * DIFF_ROW
PAGES_PER_STEP = 4


def _cached_attn_kernel(pt_ref, lam_ref, qd_ref, kn_ref, vn_ref, *rest, npairs, heads, page, npg, post_scale):
    del pt_ref
    kc_refs = rest[:npg]
    vc_refs = rest[npg:2 * npg]
    g_ref, o_ref, m_scr, l_scr, acc_scr = rest[2 * npg:]
    p = pl.program_id(1)
    nt = SUBLANES
    contract_last = (((1,), (1,)), ((), ()))

    def scores(key_tile):
        parts = [lax.dot_general(qd_ref[hp], key_tile(hp), contract_last, preferred_element_type=F32)
                 for hp in range(npairs)]
        return jnp.concatenate(parts, axis=0)

    def pair_slice(hp):
        return slice(hp * PAIR_ROWS, (hp + 1) * PAIR_ROWS)

    @pl.when(p == 0)
    def _():
        s = scores(lambda hp: kn_ref[:, hp * PAIR_COLS:(hp + 1) * PAIR_COLS])
        qpos = lax.broadcasted_iota(jnp.int32, s.shape, 0) % nt
        kpos = lax.broadcasted_iota(jnp.int32, s.shape, 1)
        s = jnp.where(kpos <= qpos, s, -jnp.inf)
        m = jnp.max(s, axis=1, keepdims=True)
        pr = jnp.exp2(s - m)
        m_scr[...] = m
        l_scr[...] = jnp.sum(pr, axis=1, keepdims=True)
        prb = pr.astype(BF16)
        for hp in range(npairs):
            acc_scr[hp] = jnp.dot(prb[pair_slice(hp)], vn_ref[:, hp * PAIR_COLS:(hp + 1) * PAIR_COLS],
                                  preferred_element_type=F32)

    def head_rows(ref, h):
        return ref[pl.ds(h, page, stride=heads), :]

    def page_tile(refs, hp):
        tiles = [jnp.concatenate([head_rows(ref, 2 * hp), head_rows(ref, 2 * hp + 1)], axis=1) for ref in refs]
        return jnp.concatenate(tiles, axis=0).astype(BF16)

    s = scores(lambda hp: page_tile(kc_refs, hp))
    m_prev = m_scr[...]
    m_new = jnp.maximum(m_prev, jnp.max(s, axis=1, keepdims=True))
    alpha = jnp.exp2(m_prev - m_new)
    pr = jnp.exp2(s - m_new)
    l_scr[...] = alpha * l_scr[...] + jnp.sum(pr, axis=1, keepdims=True)
    m_scr[...] = m_new
    prb = pr.astype(BF16)
    for hp in range(npairs):
        acc_scr[hp] = alpha[pair_slice(hp)] * acc_scr[hp] + jnp.dot(
            prb[pair_slice(hp)], page_tile(vc_refs, hp), preferred_element_type=F32)

    @pl.when(p == pl.num_programs(1) - 1)
    def _():
        lam = lam_ref[0, 0]
        l_all = l_scr[...]
        for hp in range(npairs):
            acc = acc_scr[hp]
            for hh in range(2):
                r0 = hp * PAIR_ROWS + hh * 2 * nt
                a0 = acc[hh * 2 * nt:hh * 2 * nt + nt, hh * DIFF_ROW:(hh + 1) * DIFF_ROW]
                a1 = acc[hh * 2 * nt + nt:(hh + 1) * 2 * nt, hh * DIFF_ROW:(hh + 1) * DIFF_ROW]
                a = a0 / l_all[r0:r0 + nt] - lam * (a1 / l_all[r0 + nt:r0 + 2 * nt])
                ms = jnp.mean(a * a, axis=-1, keepdims=True)
                h = 2 * hp + hh
                o_ref[:, h * DIFF_ROW:(h + 1) * DIFF_ROW] = a * lax.rsqrt(ms + RMS_EPS) * g_ref[...] * post_scale


def cached_diff_attn(q_bf, k_bf, v_bf, cache_k, cache_v, layer_j, page_table, lam, subln, post_scale):
    n, t, e = q_bf.shape
    assert t == SUBLANES
    heads = e // DIFF_ROW
    npairs = heads // 2
    n_pages = page_table.shape[1]
    n_pool, n_layers, page = cache_k.shape[:3]
    npg = math.gcd(PAGES_PER_STEP, n_pages)
    q5 = q_bf.reshape(n, t, npairs, 2, 2, DIFF_HEAD_DIM)
    eye2 = jnp.eye(2, dtype=BF16)
    qd = jnp.einsum('ntphjd,hg,jk->nphjtgkd', q5, eye2, eye2).reshape(n, npairs, PAIR_ROWS, PAIR_COLS)
    pad = ((0, 0), (0, BF16_SUBLANES - t), (0, 0))
    kn = jnp.pad(k_bf, pad)
    vn = jnp.pad(v_bf, pad)
    ck = cache_k.reshape(n_pool, n_layers, page * heads, DIFF_ROW)
    cv = cache_v.reshape(n_pool, n_layers, page * heads, DIFF_ROW)

    def cache_spec(i):
        return pl.BlockSpec((None, None, page * heads, DIFF_ROW),
                            lambda b, p, pt: (pt[b, npg * p + i], layer_j, 0, 0))

    new_spec = pl.BlockSpec((None, BF16_SUBLANES, e), lambda b, p, pt: (b, 0, 0))
    grid_spec = pltpu.PrefetchScalarGridSpec(
        num_scalar_prefetch=1,
        grid=(n, n_pages // npg),
        in_specs=[
            pl.BlockSpec(memory_space=pltpu.SMEM),
            pl.BlockSpec((None, npairs, PAIR_ROWS, PAIR_COLS), lambda b, p, pt: (b, 0, 0, 0)),
            new_spec, new_spec,
            *[cache_spec(i) for i in range(npg)],
            *[cache_spec(i) for i in range(npg)],
            pl.BlockSpec((1, DIFF_ROW), lambda b, p, pt: (0, 0)),
        ],
        out_specs=pl.BlockSpec((None, t, e), lambda b, p, pt: (b, 0, 0)),
        scratch_shapes=[
            pltpu.VMEM((npairs * PAIR_ROWS, 1), F32),
            pltpu.VMEM((npairs * PAIR_ROWS, 1), F32),
            pltpu.VMEM((npairs, PAIR_ROWS, PAIR_COLS), F32),
        ],
    )
    return pl.pallas_call(
        functools.partial(_cached_attn_kernel, npairs=npairs, heads=heads, page=page, npg=npg,
                          post_scale=post_scale),
        grid_spec=grid_spec,
        out_shape=jax.ShapeDtypeStruct((n, t, e), F32),
        compiler_params=_params("parallel", "arbitrary"),
        name="cached_diff_attn",
    )(page_table, lam.reshape(1, 1), qd, kn, vn, *([ck] * npg), *([cv] * npg), subln.reshape(1, DIFF_ROW))


def _hgrn_head(q, fz, v, lb, g, st, tc):
    dk = q.shape[1]
    f = lb + (1.0 - lb) * jax.nn.sigmoid(fz)
    logf = jnp.log(f)
    kk = (1.0 - lb) * jax.nn.sigmoid(-fz)
    row = lax.broadcasted_iota(jnp.int32, (tc, dk), 0)
    b = logf
    k = 1
    while k < tc:
        b = b + jnp.where(row >= k, pltpu.roll(b, k, 0), 0.0)
        k *= 2
    contract_last = (((1,), (1,)), ((), ()))
    contract_first = (((0,), (0,)), ((), ()))
    o = lax.dot_general((q * jnp.exp(b)).astype(BF16), st.astype(BF16), contract_last,
                        preferred_element_type=F32)
    rin = row % SUBLANES

    def shift(x, d):
        return pltpu.roll(x.reshape(tc // SUBLANES, SUBLANES, dk), d, 1).reshape(tc, dk)

    dec = None
    for d in range(SUBLANES):
        if d == 0:
            w = q * kk
            vr = v
        else:
            dec = f if d == 1 else dec * shift(f, d - 1)
            w = q * shift(kk, d) * dec
            vr = shift(v, d)
        cd = jnp.sum(jnp.where(rin >= d, w, 0.0), axis=1, keepdims=True)
        o = o + cd * vr
    if tc > SUBLANES:
        trow = lax.broadcasted_iota(jnp.int32, (tc, tc), 0)
        tcol = lax.broadcasted_iota(jnp.int32, (tc, tc), 1)
        att = jnp.zeros((tc, tc), F32)
        blk = 2 * SUBLANES
        while blk <= tc:
            half = blk // 2
            b3 = b.reshape(tc // blk, blk, dk)
            anchor = jnp.broadcast_to(b3[:, half - 1:half, :], b3.shape).reshape(tc, dk)
            pos = row % blk
            qt = jnp.where(pos >= half, q * jnp.exp(b - anchor), 0.0)
            kt = jnp.where(pos < half, kk * jnp.exp(anchor - b), 0.0)
            a = lax.dot_general(qt.astype(BF16), kt.astype(BF16), contract_last, preferred_element_type=F32)
            att = att + jnp.where((trow // blk) == (tcol // blk), a, 0.0)
            blk *= 2
        o = o + jnp.dot(att.astype(BF16), v.astype(BF16), preferred_element_type=F32)
    ms = jnp.mean(o * o, axis=-1, keepdims=True)
    o = o * lax.rsqrt(ms + RMS_EPS) * g
    b_end = b[tc - 1:, :]
    khat = kk * jnp.exp(b_end - b)
    st_new = st * jnp.exp(b_end) + lax.dot_general(v.astype(BF16), khat.astype(BF16), contract_first,
                                                   preferred_element_type=F32)
    return o, st_new


def _hgrn_kernel(q_ref, fz_ref, v_ref, lb_ref, g_ref, s0_ref, o_ref, sout_ref, st_scr, *, tc, hp):
    c = pl.program_id(2)

    @pl.when(c == 0)
    def _():
        for h in range(hp):
            st_scr[h] = s0_ref[h].T

    for h in range(hp):
        hs = slice(h * LANES, (h + 1) * LANES)
        o, st_new = _hgrn_head(q_ref[:, hs], fz_ref[:, hs], v_ref[:, hs], lb_ref[:, hs], g_ref[:, hs],
                               st_scr[h], tc)
        o_ref[:, hs] = o
        st_scr[h] = st_new

    @pl.when(c == pl.num_programs(2) - 1)
    def _():
        for h in range(hp):
            sout_ref[h] = st_scr[h].T


def hgrn_scan(qfvz, lb, g_norm, s0, tc, hp):
    n, L, e4 = qfvz.shape
    e = e4 // 4
    heads = s0.shape[1]
    dk, dv = s0.shape[2], s0.shape[3]
    assert dk == LANES and dv == LANES
    ng = heads // hp
    col = lambda base: pl.BlockSpec((None, tc, hp * dk), lambda b, h, c: (b, c, base * ng + h))
    vec = pl.BlockSpec((1, hp * dk), lambda b, h, c: (0, h))
    st = pl.BlockSpec((None, hp, dk, dv), lambda b, h, c: (b, h, 0, 0))
    return pl.pallas_call(
        functools.partial(_hgrn_kernel, tc=tc, hp=hp),
        grid=(n, ng, L // tc),
        in_specs=[col(0), col(1), col(2), vec, vec, st],
        out_specs=[pl.BlockSpec((None, tc, hp * dv), lambda b, h, c: (b, c, h)), st],
        out_shape=[jax.ShapeDtypeStruct((n, L, e), F32), jax.ShapeDtypeStruct(s0.shape, F32)],
        scratch_shapes=[pltpu.VMEM((hp, dv, dk), F32)],
        compiler_params=_params("parallel", "parallel", "arbitrary"),
        name="hgrn_scan",
    )(qfvz, qfvz, qfvz, lb.reshape(1, e), g_norm.reshape(1, e), s0)


def kernel(x_prompt, x_sample, cache_attn_k, cache_attn_v, state_ssm_re, state_ssm_im, state_hgrn, page_table, c_prompt, c_sample, ada_w, ada_b, ln_g, ln_b, ssm_w_in, ssm_a_re, ssm_a_im, ssm_b_re, ssm_b_im, ssm_c_re, ssm_c_im, ssm_d, ssm_log_dt, ssm_w_glu, ssm_b_glu, ssm_w_out, attn_w_in, attn_lam_q1, attn_lam_k1, attn_lam_q2, attn_lam_k2, attn_subln, attn_w_out, hgrn_w_in, hgrn_lb_logits, hgrn_g_norm, hgrn_w_out):
    depth = ada_w.shape[0]
    alpha = (2 * depth) ** 0.25
    past_len = page_table.shape[1] * cache_attn_k.shape[2]
    ssm_layers = [i for i in range(depth) if i % N_MIXERS == 0]
    attn_layers = [i for i in range(depth) if i % N_MIXERS == 1]
    hgrn_layers = [i for i in range(depth) if i % N_MIXERS == 2]

    n_p, n_s = x_prompt.shape[0], x_sample.shape[0]
    rows = n_p + n_s
    rows_pad = -(-rows // SUBLANES) * SUBLANES
    c_all = jnp.pad(jnp.concatenate([c_prompt, c_sample], axis=0), ((0, rows_pad - rows), (0, 0)))
    mod_all = ada_modulation(c_all, ada_w, ada_b)

    lb_cum = jnp.cumsum(jax.nn.softmax(hgrn_lb_logits.astype(F32), axis=0), axis=0)
    lb_all = lb_cum - lb_cum[0]

    s5_tabs = [s5_tables(ssm_a_re[j], ssm_a_im[j], ssm_b_re[j], ssm_b_im[j], ssm_c_re[j], ssm_c_im[j],
                         ssm_log_dt[j]) for j in range(len(ssm_layers))]
    ntile = ssm_a_re.shape[1] // S5_TILE_GROUPS
    hgrn_heads = state_hgrn.shape[2]

    def run_group(x, mod_rows, pos, sample):
        n, L, _ = x.shape
        if sample:
            blk_in = dict(bb=min(n, 128), tl=L)
            blk_out = dict(bb=min(n, 32), tl=L)
            blk_rope = dict(bb=min(n, 32), tl=L)
            s5_blk = dict(bs=min(n, 16), t=L)
            hgrn_blk = dict(tc=L, hp=hgrn_heads)
        else:
            blk_in = dict(bb=1, tl=min(L, 1024))
            blk_out = dict(bb=1, tl=min(L, 256))
            blk_rope = dict(bb=1, tl=min(L, 256))
            s5_blk = dict(bs=min(n, 4), t=min(L, 512))
            hgrn_blk = dict(tc=min(L, 128), hp=hgrn_heads // 2)
        rope_tabs = rope_tables(pos)
        ssm_re, ssm_im, kv_k, kv_v, hgrn_s = [], [], [], [], []
        for i in range(depth):
            mod = mod_all[i, mod_rows][:, None, :]
            if i % N_MIXERS == 0:
                j = ssm_layers.index(i)
                uz = mod_inproj(x, mod, ssm_w_in[j].astype(BF16), **blk_in)
                bw, cw, lam_bar = s5_tabs[j]
                if sample:
                    h0r = state_ssm_re[:, j].astype(F32).reshape(n, ntile, SUBLANES, LANES)
                    h0i = state_ssm_im[:, j].astype(F32).reshape(n, ntile, SUBLANES, LANES)
                else:
                    h0r = jnp.zeros((n, ntile, SUBLANES, LANES), F32)
                    h0i = h0r
                y, hr, hi = s5_scan(uz, bw, cw, lam_bar, h0r, h0i, **s5_blk)
                x = s5_out_ln(y, uz, ssm_d[j], ssm_w_glu[j].astype(BF16), ssm_b_glu[j], x, mod,
                              ssm_w_out[j].astype(BF16), ln_g[i], ln_b[i], alpha, **blk_out)
                ssm_re.append(hr.reshape(n, -1, SSM_STATE))
                ssm_im.append(hi.reshape(n, -1, SSM_STATE))
            elif i % N_MIXERS == 1:
                j = attn_layers.index(i)
                lam_init = 0.8 - 0.6 * math.exp(-0.3 * i)
                lam = (jnp.exp(jnp.sum(attn_lam_q1[j].astype(F32) * attn_lam_k1[j].astype(F32)))
                       - jnp.exp(jnp.sum(attn_lam_q2[j].astype(F32) * attn_lam_k2[j].astype(F32)))
                       + lam_init)
                qkvz = mod_inproj(x, mod, attn_w_in[j].astype(BF16), **blk_in)
                e = qkvz.shape[-1] // 4
                heads = e // DIFF_ROW
                q_bf, k_f, k_bf, v_f, v_op = rope_qkv(qkvz, rope_tabs, transpose_v=not sample, **blk_rope)
                if sample:
                    a = cached_diff_attn(q_bf, k_bf, v_op, cache_attn_k, cache_attn_v, j, page_table, lam,
                                         attn_subln[j], 1.0 - lam_init)
                else:
                    tq = min(L, 1024)
                    a = flash_diff_attn(q_bf, k_bf, v_op, lam, attn_subln[j], 1.0 - lam_init, tq=tq, tk=tq)
                x = out_ln(a, qkvz, 3, x, mod, attn_w_out[j].astype(BF16), ln_g[i], ln_b[i], alpha, **blk_out)
                kv_k.append(k_f.reshape(n, L, heads, DIFF_ROW))
                kv_v.append(v_f.reshape(n, L, heads, DIFF_ROW))
            else:
                j = hgrn_layers.index(i)
                qfvz = mod_inproj(x, mod, hgrn_w_in[j].astype(BF16), **blk_in)
                if sample:
                    s0 = state_hgrn[:, j].astype(F32)
                else:
                    s0 = jnp.zeros((n,) + state_hgrn.shape[2:], F32)
                o, s_fin = hgrn_scan(qfvz, lb_all[i], hgrn_g_norm[j].astype(F32), s0, **hgrn_blk)
                x = out_ln(o, qfvz, 3, x, mod, hgrn_w_out[j].astype(BF16), ln_g[i], ln_b[i], alpha, **blk_out)
                hgrn_s.append(s_fin)
        return (x, jnp.stack(kv_k, axis=2), jnp.stack(kv_v, axis=2),
                jnp.stack(ssm_re, axis=1), jnp.stack(ssm_im, axis=1), jnp.stack(hgrn_s, axis=1))

    pos_p = jnp.arange(x_prompt.shape[1], dtype=jnp.int32)
    pos_s = past_len + jnp.arange(x_sample.shape[1], dtype=jnp.int32)
    y_p, k_p, v_p, sr_p, si_p, hs_p = run_group(x_prompt, slice(0, n_p), pos_p, False)
    y_s, k_s, v_s, sr_s, si_s, hs_s = run_group(x_sample, slice(n_p, n_p + n_s), pos_s, True)
    return (y_p, y_s, k_p, v_p, sr_p, si_p, hs_p, k_s, v_s, sr_s, si_s, hs_s)
```

```python
import functools
import math

import jax
import jax.numpy as jnp
from jax import lax
from jax.experimental import pallas as pl
from jax.experimental.pallas import tpu as pltpu

F32 = jnp.float32
BF16 = jnp.bfloat16

LANES = 128
SUBLANES = 8
BF16_SUBLANES = 16
VMEM_LIMIT_BYTES = 56 * 1024 * 1024

N_MIXERS = 3
SSM_GROUP = 16
SSM_STATE = 64
DIFF_HEAD_DIM = 64
DIFF_ROW = 2 * DIFF_HEAD_DIM
ROPE_DIM = DIFF_HEAD_DIM // 4
ROPE_THETA = 500000.0
LN_EPS = 1e-5
RMS_EPS = 1e-6
LOG2E = 1.4426950408889634

S5_TILE_GROUPS = 16
S5_TILE_CH = S5_TILE_GROUPS * SSM_GROUP
S5_TILE_STATES = S5_TILE_GROUPS * SSM_STATE
S5_CHUNKS = S5_TILE_STATES // LANES
assert S5_CHUNKS == SUBLANES

VT_ROWS = DIFF_ROW + BF16_SUBLANES


def _params(*sem):
    return pltpu.CompilerParams(dimension_semantics=sem, vmem_limit_bytes=VMEM_LIMIT_BYTES)


def _silu(x):
    return x * jax.nn.sigmoid(x)


def _ada_kernel(c_ref, w_ref, b_ref, o_ref):
    s = _silu(c_ref[...]).astype(BF16)
    o_ref[...] = jnp.dot(s, w_ref[...].astype(BF16), preferred_element_type=F32) + b_ref[...]


def ada_modulation(c_all, ada_w, ada_b):
    depth, d, d3 = ada_w.shape
    rows = c_all.shape[0]
    tn = 1024
    return pl.pallas_call(
        _ada_kernel,
        grid=(depth, d3 // tn),
        in_specs=[
            pl.BlockSpec((rows, d), lambda i, j: (0, 0)),
            pl.BlockSpec((None, d, tn), lambda i, j: (i, 0, j)),
            pl.BlockSpec((None, 1, tn), lambda i, j: (i, 0, j)),
        ],
        out_specs=pl.BlockSpec((None, rows, tn), lambda i, j: (i, 0, j)),
        out_shape=jax.ShapeDtypeStruct((depth, rows, d3), F32),
        compiler_params=_params("parallel", "parallel"),
        name="ada_modulation",
    )(c_all, ada_w, ada_b.reshape(depth, 1, d3))


def _inproj_kernel(x_ref, shift_ref, scale_ref, w_ref, o_ref):
    h = x_ref[...] * (1.0 + scale_ref[...]) + shift_ref[...]
    bb, tl, d = h.shape
    o = jnp.dot(h.reshape(bb * tl, d).astype(BF16), w_ref[...], preferred_element_type=F32)
    o_ref[...] = o.reshape(bb, tl, o.shape[-1])


def mod_inproj(x, mod, w_bf16, bb, tl, tn=1024):
    n, L, d = x.shape
    nout = w_bf16.shape[1]
    return pl.pallas_call(
        _inproj_kernel,
        grid=(n // bb, L // tl, nout // tn),
        in_specs=[
            pl.BlockSpec((bb, tl, d), lambda b, l, j: (b, l, 0)),
            pl.BlockSpec((bb, 1, d), lambda b, l, j: (b, 0, 0)),
            pl.BlockSpec((bb, 1, d), lambda b, l, j: (b, 0, 1)),
            pl.BlockSpec((d, tn), lambda b, l, j: (0, j)),
        ],
        out_specs=pl.BlockSpec((bb, tl, tn), lambda b, l, j: (b, l, j)),
        out_shape=jax.ShapeDtypeStruct((n, L, nout), F32),
        compiler_params=_params("parallel", "parallel", "arbitrary"),
        name="mod_inproj",
    )(x, mod, mod, w_bf16)


def _out_ln_kernel(a_ref, z_ref, x_ref, gate_ref, w_ref, g_ref, b_ref, o_ref, *, alpha):
    act = a_ref[...] * _silu(z_ref[...])
    bb, tl, e = act.shape
    out = jnp.dot(act.reshape(bb * tl, e).astype(BF16), w_ref[...], preferred_element_type=F32)
    out = out.reshape(bb, tl, out.shape[-1])
    r = alpha * x_ref[...] + (1.0 + gate_ref[...]) * out
    mu = jnp.mean(r, axis=-1, keepdims=True)
    dlt = r - mu
    var = jnp.mean(dlt * dlt, axis=-1, keepdims=True)
    o_ref[...] = dlt * lax.rsqrt(var + LN_EPS) * g_ref[...] + b_ref[...]


def out_ln(a, zsrc, zblk, x, mod, w_bf16, ln_g, ln_b, alpha, bb, tl):
    n, L, d = x.shape
    e = a.shape[-1]
    return pl.pallas_call(
        functools.partial(_out_ln_kernel, alpha=alpha),
        grid=(n // bb, L // tl),
        in_specs=[
            pl.BlockSpec((bb, tl, e), lambda b, l: (b, l, 0)),
            pl.BlockSpec((bb, tl, e), lambda b, l: (b, l, zblk)),
            pl.BlockSpec((bb, tl, d), lambda b, l: (b, l, 0)),
            pl.BlockSpec((bb, 1, d), lambda b, l: (b, 0, 2)),
            pl.BlockSpec((e, d), lambda b, l: (0, 0)),
            pl.BlockSpec((1, d), lambda b, l: (0, 0)),
            pl.BlockSpec((1, d), lambda b, l: (0, 0)),
        ],
        out_specs=pl.BlockSpec((bb, tl, d), lambda b, l: (b, l, 0)),
        out_shape=jax.ShapeDtypeStruct((n, L, d), F32),
        compiler_params=_params("parallel", "parallel"),
        name="out_ln",
    )(a, zsrc, x, mod, w_bf16, ln_g.reshape(1, d), ln_b.reshape(1, d))


def _s5_scan_kernel(u_ref, bw_ref, cw_ref, lam_ref, h0r_ref, h0i_ref, y_ref, hr_ref, hi_ref,
                    xr_scr, xi_scr, car_scr, *, bs, t):
    c = pl.program_id(2)
    nst = S5_TILE_STATES
    u = u_ref[...].reshape(bs * t, S5_TILE_CH).astype(BF16)
    bu = jnp.dot(u, bw_ref[...], preferred_element_type=F32)
    for s in range(bs):
        rows = slice(s * t, (s + 1) * t)
        for a in range(S5_CHUNKS):
            xr_scr[s, pl.ds(a, t, stride=SUBLANES), :] = bu[rows, a * LANES:(a + 1) * LANES]
            xi_scr[s, pl.ds(a, t, stride=SUBLANES), :] = bu[rows, nst + a * LANES:nst + (a + 1) * LANES]

    @pl.when(c == 0)
    def _():
        car_scr[0] = h0r_ref[...]
        car_scr[1] = h0i_ref[...]

    lr = lam_ref[0]
    li = lam_ref[1]

    def body(p, carry):
        r0 = pl.multiple_of(p * SUBLANES, SUBLANES)
        out = []
        for s in range(bs):
            hr, hi = carry[2 * s], carry[2 * s + 1]
            nr = lr * hr - li * hi + xr_scr[s, pl.ds(r0, SUBLANES), :]
            ni = lr * hi + li * hr + xi_scr[s, pl.ds(r0, SUBLANES), :]
            xr_scr[s, pl.ds(r0, SUBLANES), :] = nr
            xi_scr[s, pl.ds(r0, SUBLANES), :] = ni
            out += [nr, ni]
        return tuple(out)

    init = []
    for s in range(bs):
        init += [car_scr[0, s], car_scr[1, s]]
    fin = lax.fori_loop(0, t, body, tuple(init), unroll=SUBLANES)
    for s in range(bs):
        car_scr[0, s] = fin[2 * s]
        car_scr[1, s] = fin[2 * s + 1]

    for s in range(bs):
        parts = [xr_scr[s, pl.ds(a, t, stride=SUBLANES), :] for a in range(S5_CHUNKS)]
        parts += [xi_scr[s, pl.ds(a, t, stride=SUBLANES), :] for a in range(S5_CHUNKS)]
        hcat = jnp.concatenate(parts, axis=1).astype(BF16)
        y_ref[s] = jnp.dot(hcat, cw_ref[...], preferred_element_type=F32)

    @pl.when(c == pl.num_programs(2) - 1)
    def _():
        hr_ref[...] = car_scr[0]
        hi_ref[...] = car_scr[1]


def s5_scan(uz, bw, cw, lam, h0r, h0i, bs, t):
    n, L, _ = uz.shape
    ntile = bw.shape[0]
    nst = S5_TILE_STATES
    st_spec = pl.BlockSpec((bs, None, SUBLANES, LANES), lambda b, j, c: (b, j, 0, 0))
    return pl.pallas_call(
        functools.partial(_s5_scan_kernel, bs=bs, t=t),
        grid=(n // bs, ntile, L // t),
        in_specs=[
            pl.BlockSpec((bs, t, S5_TILE_CH), lambda b, j, c: (b, c, j)),
            pl.BlockSpec((None, S5_TILE_CH, 2 * nst), lambda b, j, c: (j, 0, 0)),
            pl.BlockSpec((None, 2 * nst, S5_TILE_CH), lambda b, j, c: (j, 0, 0)),
            pl.BlockSpec((None, 2, SUBLANES, LANES), lambda b, j, c: (j, 0, 0, 0)),
            st_spec, st_spec,
        ],
        out_specs=[
            pl.BlockSpec((bs, t, S5_TILE_CH), lambda b, j, c: (b, c, j)),
            st_spec, st_spec,
        ],
        out_shape=[
            jax.ShapeDtypeStruct((n, L, ntile * S5_TILE_CH), F32),
            jax.ShapeDtypeStruct((n, ntile, SUBLANES, LANES), F32),
            jax.ShapeDtypeStruct((n, ntile, SUBLANES, LANES), F32),
        ],
        scratch_shapes=[
            pltpu.VMEM((bs, t * SUBLANES, LANES), F32),
            pltpu.VMEM((bs, t * SUBLANES, LANES), F32),
            pltpu.VMEM((2, bs, SUBLANES, LANES), F32),
        ],
        compiler_params=_params("parallel", "parallel", "arbitrary"),
        name="s5_scan",
    )(uz, bw, cw, lam, h0r, h0i)


def _s5_out_kernel(y_ref, u_ref, z_ref, d_ref, wg_ref, bg_ref, x_ref, gate_ref, w_ref, g_ref, b_ref, o_ref, *,
                   alpha):
    y = jax.nn.gelu(y_ref[...] + d_ref[...] * u_ref[...])
    bb, tl, e = y.shape
    y2 = y.reshape(bb * tl, e)
    glu = jnp.dot(y2.astype(BF16), wg_ref[...], preferred_element_type=F32) + bg_ref[...]
    act = y2 * jax.nn.sigmoid(glu) * _silu(z_ref[...].reshape(bb * tl, e))
    out = jnp.dot(act.astype(BF16), w_ref[...], preferred_element_type=F32)
    out = out.reshape(bb, tl, out.shape[-1])
    r = alpha * x_ref[...] + (1.0 + gate_ref[...]) * out
    mu = jnp.mean(r, axis=-1, keepdims=True)
    dlt = r - mu
    var = jnp.mean(dlt * dlt, axis=-1, keepdims=True)
    o_ref[...] = dlt * lax.rsqrt(var + LN_EPS) * g_ref[...] + b_ref[...]


def s5_out_ln(y, uz, d_skip, w_glu_bf16, b_glu, x, mod, w_bf16, ln_g, ln_b, alpha, bb, tl):
    n, L, d = x.shape
    e = y.shape[-1]
    return pl.pallas_call(
        functools.partial(_s5_out_kernel, alpha=alpha),
        grid=(n // bb, L // tl),
        in_specs=[
            pl.BlockSpec((bb, tl, e), lambda b, l: (b, l, 0)),
            pl.BlockSpec((bb, tl, e), lambda b, l: (b, l, 0)),
            pl.BlockSpec((bb, tl, e), lambda b, l: (b, l, 1)),
            pl.BlockSpec((1, e), lambda b, l: (0, 0)),
            pl.BlockSpec((e, e), lambda b, l: (0, 0)),
            pl.BlockSpec((1, e), lambda b, l: (0, 0)),
            pl.BlockSpec((bb, tl, d), lambda b, l: (b, l, 0)),
            pl.BlockSpec((bb, 1, d), lambda b, l: (b, 0, 2)),
            pl.BlockSpec((e, d), lambda b, l: (0, 0)),
            pl.BlockSpec((1, d), lambda b, l: (0, 0)),
            pl.BlockSpec((1, d), lambda b, l: (0, 0)),
        ],
        out_specs=pl.BlockSpec((bb, tl, d), lambda b, l: (b, l, 0)),
        out_shape=jax.ShapeDtypeStruct((n, L, d), F32),
        compiler_params=_params("parallel", "parallel"),
        name="s5_out_ln",
    )(y, uz, uz, d_skip.reshape(1, e), w_glu_bf16, b_glu.reshape(1, e), x, mod, w_bf16,
      ln_g.reshape(1, d), ln_b.reshape(1, d))


def s5_tables(a_re, a_im, b_re, b_im, c_re, c_im, log_dt):
    g, p = a_re.shape
    ntile = g // S5_TILE_GROUPS
    dt = jnp.exp(log_dt.astype(F32))[:, None]
    lr, li = a_re.astype(F32), a_im.astype(F32)
    mag = jnp.exp(lr * dt)
    lb_re, lb_im = mag * jnp.cos(li * dt), mag * jnp.sin(li * dt)
    nr, ni = lb_re - 1.0, lb_im
    den = lr * lr + li * li
    w_re = (nr * lr + ni * li) / den
    w_im = (ni * lr - nr * li) / den
    br, bi = b_re.astype(F32), b_im.astype(F32)
    bb_re = w_re[..., None] * br - w_im[..., None] * bi
    bb_im = w_re[..., None] * bi + w_im[..., None] * br
    eye = jnp.eye(S5_TILE_GROUPS, dtype=F32)

    def in_layout(bb):
        t4 = bb.reshape(ntile, S5_TILE_GROUPS, p, SSM_GROUP)
        return jnp.einsum('jgpc,gh->jgchp', t4, eye).reshape(ntile, S5_TILE_CH, S5_TILE_STATES)

    def out_layout(cc):
        t4 = cc.reshape(ntile, S5_TILE_GROUPS, SSM_GROUP, p)
        return jnp.einsum('jgcp,gh->jgphc', t4, eye).reshape(ntile, S5_TILE_STATES, S5_TILE_CH)

    bw = jnp.concatenate([in_layout(bb_re), in_layout(bb_im)], axis=2).astype(BF16)
    cw = jnp.concatenate([out_layout(c_re.astype(F32)), -out_layout(c_im.astype(F32))], axis=1).astype(BF16)
    lam = jnp.stack([lb_re.reshape(ntile, SUBLANES, LANES), lb_im.reshape(ntile, SUBLANES, LANES)], axis=1)
    return bw, cw, lam


def _rope_kernel(q_ref, k_ref, v_ref, cos_ref, sa_ref, sb_ref, qo_ref, kf_ref, kb_ref, vf_ref, vo_ref, *,
                 qscale, transpose_v):
    cos = cos_ref[...][None]
    sa = sa_ref[...][None]
    sb = sb_ref[...][None]
    bb, tl, e = q_ref.shape

    def rot(x):
        x2 = x.reshape(bb * tl, LANES)
        up = pltpu.roll(x2, LANES - SUBLANES, 1).reshape(bb, tl, LANES)
        dn = pltpu.roll(x2, SUBLANES, 1).reshape(bb, tl, LANES)
        return x * cos + up * sa + dn * sb

    for tile in range(e // LANES):
        sl = slice(tile * LANES, (tile + 1) * LANES)
        qo_ref[:, :, sl] = (rot(q_ref[:, :, sl]) * qscale).astype(BF16)
        kr = rot(k_ref[:, :, sl])
        kf_ref[:, :, sl] = kr
        kb_ref[:, :, sl] = kr.astype(BF16)
        if transpose_v:
            vo_ref[tile, :DIFF_ROW, :] = v_ref[0, :, sl].T.astype(BF16)
            vo_ref[tile, DIFF_ROW:, :] = jnp.ones((VT_ROWS - DIFF_ROW, tl), BF16)
    if not transpose_v:
        vo_ref[...] = v_ref[...].astype(BF16)
    vf_ref[...] = v_ref[...]


def rope_tables(pos):
    half = ROPE_DIM // 2
    assert half == SUBLANES
    inv = ROPE_THETA ** (-jnp.arange(half, dtype=F32) / half)
    ang = pos.astype(F32)[:, None] * inv[None, :]
    cos, sin = jnp.cos(ang), jnp.sin(ang)
    lane = jnp.arange(LANES) % DIFF_HEAD_DIM
    idx = lane % half
    first = (lane < half)[None, :]
    second = ((lane >= half) & (lane < ROPE_DIM))[None, :]
    cos_t = jnp.where(first | second, cos[:, idx], 1.0)
    sa_t = jnp.where(first, -sin[:, idx], 0.0)
    sb_t = jnp.where(second, sin[:, idx], 0.0)
    return cos_t, sa_t, sb_t


def rope_qkv(qkvz, tables, bb, tl, transpose_v):
    n, L, e4 = qkvz.shape
    e = e4 // 4
    heads = e // DIFF_ROW
    cos_t, sa_t, sb_t = tables
    tab = pl.BlockSpec((tl, LANES), lambda b, l: (l, 0))
    blk = lambda col: pl.BlockSpec((bb, tl, e), lambda b, l: (b, l, col))
    if transpose_v:
        assert bb == 1
        v_spec = pl.BlockSpec((None, heads, VT_ROWS, tl), lambda b, l: (b, 0, 0, l))
        v_shape = jax.ShapeDtypeStruct((n, heads, VT_ROWS, L), BF16)
    else:
        v_spec = blk(0)
        v_shape = jax.ShapeDtypeStruct((n, L, e), BF16)
    return pl.pallas_call(
        functools.partial(_rope_kernel, qscale=LOG2E * DIFF_HEAD_DIM ** -0.5, transpose_v=transpose_v),
        grid=(n // bb, L // tl),
        in_specs=[blk(0), blk(1), blk(2), tab, tab, tab],
        out_specs=[blk(0), blk(0), blk(0), blk(0), v_spec],
        out_shape=[
            jax.ShapeDtypeStruct((n, L, e), BF16),
            jax.ShapeDtypeStruct((n, L, e), F32),
            jax.ShapeDtypeStruct((n, L, e), BF16),
            jax.ShapeDtypeStruct((n, L, e), F32),
            v_shape,
        ],
        compiler_params=_params("parallel", "parallel"),
        name="rope_qkv",
    )(qkvz, qkvz, qkvz, cos_t, sa_t, sb_t)


def _flash_kernel(lam_ref, q_ref, k_ref, vt_ref, g_ref, o_ref, m_scr, acc_scr, s_scr, *, tq, tk, post_scale):
    i = pl.program_id(2)
    q = q_ref[...]
    lane = lax.broadcasted_iota(jnp.int32, q.shape, 1)
    zero = jnp.zeros_like(q)
    qmaps = (jnp.where(lane < DIFF_HEAD_DIM, q, zero), jnp.where(lane >= DIFF_HEAD_DIM, q, zero))
    contract_last = (((1,), (1,)), ((), ()))

    m_scr[...] = jnp.full(m_scr.shape, -jnp.inf, F32)
    acc_scr[...] = jnp.zeros(acc_scr.shape, F32)

    def block(j, mask):
        k0 = pl.multiple_of(j * tk, tk)
        k = k_ref[pl.ds(k0, tk), :]
        vt = vt_ref[:, pl.ds(k0, tk)]
        for mp in range(2):
            s = lax.dot_general(k, qmaps[mp], contract_last, preferred_element_type=F32)
            if mask is not None:
                s = jnp.where(mask, s, -jnp.inf)
            s_scr[mp] = s
            m_prev = m_scr[mp]
            m_new = jnp.maximum(m_prev, jnp.max(s_scr[mp], axis=0, keepdims=True))
            alpha = jnp.exp2(m_prev - m_new)
            p = jnp.exp2(s_scr[mp] - m_new).astype(BF16)
            acc_scr[mp] = alpha * acc_scr[mp] + jnp.dot(vt, p, preferred_element_type=F32)
            m_scr[mp] = m_new

    def body(j, carry):
        block(j, None)
        return carry

    sub = tq // tk
    lax.fori_loop(0, i * sub, body, 0)
    keys = lax.broadcasted_iota(jnp.int32, (tk, tq), 0)
    qrys = lax.broadcasted_iota(jnp.int32, (tk, tq), 1)
    for d in range(sub):
        block(i * sub + d, keys + d * tk <= qrys)

    lam = lam_ref[0, 0]
    acc0 = acc_scr[0]
    acc1 = acc_scr[1]
    a = (acc0[:DIFF_ROW] * (1.0 / acc0[DIFF_ROW:DIFF_ROW + 1])
         - lam * (acc1[:DIFF_ROW] * (1.0 / acc1[DIFF_ROW:DIFF_ROW + 1])))
    ms = jnp.mean(a * a, axis=0, keepdims=True)
    a = a * lax.rsqrt(ms + RMS_EPS)
    o_ref[...] = a.T * g_ref[...] * post_scale


def flash_diff_attn(q, k, vt, lam, subln, post_scale, tq, tk):
    n, L, e = q.shape
    heads = e // DIFF_ROW
    return pl.pallas_call(
        functools.partial(_flash_kernel, tq=tq, tk=tk, post_scale=post_scale),
        grid=(n, heads, L // tq),
        in_specs=[
            pl.BlockSpec(memory_space=pltpu.SMEM),
            pl.BlockSpec((None, tq, DIFF_ROW), lambda b, h, i: (b, i, h)),
            pl.BlockSpec((None, L, DIFF_ROW), lambda b, h, i: (b, 0, h)),
            pl.BlockSpec((None, None, VT_ROWS, L), lambda b, h, i: (b, h, 0, 0)),
            pl.BlockSpec((1, DIFF_ROW), lambda b, h, i: (0, 0)),
        ],
        out_specs=pl.BlockSpec((None, tq, DIFF_ROW), lambda b, h, i: (b, i, h)),
        out_shape=jax.ShapeDtypeStruct((n, L, e), F32),
        scratch_shapes=[
            pltpu.VMEM((2, 1, tq), F32),
            pltpu.VMEM((2, VT_ROWS, tq), F32),
            pltpu.VMEM((2, tk, tq), F32),
        ],
        compiler_params=_params("parallel", "parallel", "arbitrary"),
        name="flash_diff_attn",
    )(lam.reshape(1, 1), q, k, vt, subln.reshape(1, DIFF_ROW))


PAIR_ROWS = 4 * SUBLANES
PAIR_COLS = 2 * DIFF_ROW
PAGES_PER_STEP = 4


def _cached_attn_kernel(pt_ref, lam_ref, qd_ref, kn_ref, vn_ref, *rest, npairs, heads, page, npg, post_scale):
    del pt_ref
    kc_refs = rest[:npg]
    vc_refs = rest[npg:2 * npg]
    g_ref, o_ref, m_scr, l_scr, acc_scr = rest[2 * npg:]
    p = pl.program_id(1)
    nt = SUBLANES
    contract_last = (((1,), (1,)), ((), ()))

    def scores(key_tile):
        parts = [lax.dot_general(qd_ref[hp], key_tile(hp), contract_last, preferred_element_type=F32)
                 for hp in range(npairs)]
        return jnp.concatenate(parts, axis=0)

    def pair_slice(hp):
        return slice(hp * PAIR_ROWS, (hp + 1) * PAIR_ROWS)

    @pl.when(p == 0)
    def _():
        s = scores(lambda hp: kn_ref[:, hp * PAIR_COLS:(hp + 1) * PAIR_COLS])
        qpos = lax.broadcasted_iota(jnp.int32, s.shape, 0) % nt
        kpos = lax.broadcasted_iota(jnp.int32, s.shape, 1)
        s = jnp.where(kpos <= qpos, s, -jnp.inf)
        m = jnp.max(s, axis=1, keepdims=True)
        pr = jnp.exp2(s - m)
        m_scr[...] = m
        l_scr[...] = jnp.sum(pr, axis=1, keepdims=True)
        prb = pr.astype(BF16)
        for hp in range(npairs):
            acc_scr[hp] = jnp.dot(prb[pair_slice(hp)], vn_ref[:, hp * PAIR_COLS:(hp + 1) * PAIR_COLS],
                                  preferred_element_type=F32)

    def head_rows(ref, h):
        return ref[pl.ds(h, page, stride=heads), :]

    def page_tile(refs, hp):
        tiles = [jnp.concatenate([head_rows(ref, 2 * hp), head_rows(ref, 2 * hp + 1)], axis=1) for ref in refs]
        return jnp.concatenate(tiles, axis=0).astype(BF16)

    s = scores(lambda hp: page_tile(kc_refs, hp))
    m_prev = m_scr[...]
    m_new = jnp.maximum(m_prev, jnp.max(s, axis=1, keepdims=True))
    alpha = jnp.exp2(m_prev - m_new)
    pr = jnp.exp2(s - m_new)
    l_scr[...] = alpha * l_scr[...] + jnp.sum(pr, axis=1, keepdims=True)
    m_scr[...] = m_new
    prb = pr.astype(BF16)
    for hp in range(npairs):
        acc_scr[hp] = alpha[pair_slice(hp)] * acc_scr[hp] + jnp.dot(
            prb[pair_slice(hp)], page_tile(vc_refs, hp), preferred_element_type=F32)

    @pl.when(p == pl.num_programs(1) - 1)
    def _():
        lam = lam_ref[0, 0]
        l_all = l_scr[...]
        for hp in range(npairs):
            acc = acc_scr[hp]
            for hh in range(2):
                r0 = hp * PAIR_ROWS + hh * 2 * nt
                a0 = acc[hh * 2 * nt:hh * 2 * nt + nt, hh * DIFF_ROW:(hh + 1) * DIFF_ROW]
                a1 = acc[hh * 2 * nt + nt:(hh + 1) * 2 * nt, hh * DIFF_ROW:(hh + 1) * DIFF_ROW]
                a = a0 / l_all[r0:r0 + nt] - lam * (a1 / l_all[r0 + nt:r0 + 2 * nt])
                ms = jnp.mean(a * a, axis=-1, keepdims=True)
                h = 2 * hp + hh
                o_ref[:, h * DIFF_ROW:(h + 1) * DIFF_ROW] = a * lax.rsqrt(ms + RMS_EPS) * g_ref[...] * post_scale


def cached_diff_attn(q_bf, k_bf, v_bf, cache_k, cache_v, layer_j, page_table, lam, subln, post_scale):
    n, t, e = q_bf.shape
    assert t == SUBLANES
    heads = e // DIFF_ROW
    npairs = heads // 2
    n_pages = page_table.shape[1]
    n_pool, n_layers, page = cache_k.shape[:3]
    npg = math.gcd(PAGES_PER_STEP, n_pages)
    q5 = q_bf.reshape(n, t, npairs, 2, 2, DIFF_HEAD_DIM)
    eye2 = jnp.eye(2, dtype=BF16)
    qd = jnp.einsum('ntphjd,hg,jk->nphjtgkd', q5, eye2, eye2).reshape(n, npairs, PAIR_ROWS, PAIR_COLS)
    pad = ((0, 0), (0, BF16_SUBLANES - t), (0, 0))
    kn = jnp.pad(k_bf, pad)
    vn = jnp.pad(v_bf, pad)
    ck = cache_k.reshape(n_pool, n_layers, page * heads, DIFF_ROW)
    cv = cache_v.reshape(n_pool, n_layers, page * heads, DIFF_ROW)

    def cache_spec(i):
        return pl.BlockSpec((None, None, page * heads, DIFF_ROW),
                            lambda b, p, pt: (pt[b, npg * p + i], layer_j, 0, 0))

    new_spec = pl.BlockSpec((None, BF16_SUBLANES, e), lambda b, p, pt: (b, 0, 0))
    grid_spec = pltpu.PrefetchScalarGridSpec(
        num_scalar_prefetch=1,
        grid=(n, n_pages // npg),
        in_specs=[
            pl.BlockSpec(memory_space=pltpu.SMEM),
            pl.BlockSpec((None, npairs, PAIR_ROWS, PAIR_COLS), lambda b, p, pt: (b, 0, 0, 0)),
            new_spec, new_spec,
            *[cache_spec(i) for i in range(npg)],
            *[cache_spec(i) for i in range(npg)],
            pl.BlockSpec((1, DIFF_ROW), lambda b, p, pt: (0, 0)),
        ],
        out_specs=pl.BlockSpec((None, t, e), lambda b, p, pt: (b, 0, 0)),
        scratch_shapes=[
            pltpu.VMEM((npairs * PAIR_ROWS, 1), F32),
            pltpu.VMEM((npairs * PAIR_ROWS, 1), F32),
            pltpu.VMEM((npairs, PAIR_ROWS, PAIR_COLS), F32),
        ],
    )
    return pl.pallas_call(
        functools.partial(_cached_attn_kernel, npairs=npairs, heads=heads, page=page, npg=npg,
                          post_scale=post_scale),
        grid_spec=grid_spec,
        out_shape=jax.ShapeDtypeStruct((n, t, e), F32),
        compiler_params=_params("parallel", "arbitrary"),
        name="cached_diff_attn",
    )(page_table, lam.reshape(1, 1), qd, kn, vn, *([ck] * npg), *([cv] * npg), subln.reshape(1, DIFF_ROW))


def _hgrn_head(q, fz, v, lb, g, st, tc):
    dk = q.shape[1]
    f = lb + (1.0 - lb) * jax.nn.sigmoid(fz)
    logf = jnp.log(f)
    kk = (1.0 - lb) * jax.nn.sigmoid(-fz)
    row = lax.broadcasted_iota(jnp.int32, (tc, dk), 0)
    b = logf
    k = 1
    while k < tc:
        b = b + jnp.where(row >= k, pltpu.roll(b, k, 0), 0.0)
        k *= 2
    contract_last = (((1,), (1,)), ((), ()))
    contract_first = (((0,), (0,)), ((), ()))
    o = lax.dot_general((q * jnp.exp(b)).astype(BF16), st.astype(BF16), contract_last,
                        preferred_element_type=F32)
    rin = row % SUBLANES

    def shift(x, d):
        return pltpu.roll(x.reshape(tc // SUBLANES, SUBLANES, dk), d, 1).reshape(tc, dk)

    dec = None
    for d in range(SUBLANES):
        if d == 0:
            w = q * kk
            vr = v
        else:
            dec = f if d == 1 else dec * shift(f, d - 1)
            w = q * shift(kk, d) * dec
            vr = shift(v, d)
        cd = jnp.sum(jnp.where(rin >= d, w, 0.0), axis=1, keepdims=True)
        o = o + cd * vr
    if tc > SUBLANES:
        trow = lax.broadcasted_iota(jnp.int32, (tc, tc), 0)
        tcol = lax.broadcasted_iota(jnp.int32, (tc, tc), 1)
        att = jnp.zeros((tc, tc), F32)
        blk = 2 * SUBLANES
        while blk <= tc:
            half = blk // 2
            b3 = b.reshape(tc // blk, blk, dk)
            anchor = jnp.broadcast_to(b3[:, half - 1:half, :], b3.shape).reshape(tc, dk)
            pos = row % blk
            qt = jnp.where(pos >= half, q * jnp.exp(b - anchor), 0.0)
            kt = jnp.where(pos < half, kk * jnp.exp(anchor - b), 0.0)
            a = lax.dot_general(qt.astype(BF16), kt.astype(BF16), contract_last, preferred_element_type=F32)
            att = att + jnp.where((trow // blk) == (tcol // blk), a, 0.0)
            blk *= 2
        o = o + jnp.dot(att.astype(BF16), v.astype(BF16), preferred_element_type=F32)
    ms = jnp.mean(o * o, axis=-1, keepdims=True)
    o = o * lax.rsqrt(ms + RMS_EPS) * g
    b_end = b[tc - 1:, :]
    khat = kk * jnp.exp(b_end - b)
    st_new = st * jnp.exp(b_end) + lax.dot_general(v.astype(BF16), khat.astype(BF16), contract_first,
                                                   preferred_element_type=F32)
    return o, st_new


def _hgrn_kernel(q_ref, fz_ref, v_ref, lb_ref, g_ref, s0_ref, o_ref, sout_ref, st_scr, *, tc, hp):
    c = pl.program_id(2)

    @pl.when(c == 0)
    def _():
        for h in range(hp):
            st_scr[h] = s0_ref[h].T

    for h in range(hp):
        hs = slice(h * LANES, (h + 1) * LANES)
        o, st_new = _hgrn_head(q_ref[:, hs], fz_ref[:, hs], v_ref[:, hs], lb_ref[:, hs], g_ref[:, hs],
                               st_scr[h], tc)
        o_ref[:, hs] = o
        st_scr[h] = st_new

    @pl.when(c == pl.num_programs(2) - 1)
    def _():
        for h in range(hp):
            sout_ref[h] = st_scr[h].T


def hgrn_scan(qfvz, lb, g_norm, s0, tc, hp):
    n, L, e4 = qfvz.shape
    e = e4 // 4
    heads = s0.shape[1]
    dk, dv = s0.shape[2], s0.shape[3]
    assert dk == LANES and dv == LANES
    ng = heads // hp
    col = lambda base: pl.BlockSpec((None, tc, hp * dk), lambda b, h, c: (b, c, base * ng + h))
    vec = pl.BlockSpec((1, hp * dk), lambda b, h, c: (0, h))
    st = pl.BlockSpec((None, hp, dk, dv), lambda b, h, c: (b, h, 0, 0))
    return pl.pallas_call(
        functools.partial(_hgrn_kernel, tc=tc, hp=hp),
        grid=(n, ng, L // tc),
        in_specs=[col(0), col(1), col(2), vec, vec, st],
        out_specs=[pl.BlockSpec((None, tc, hp * dv), lambda b, h, c: (b, c, h)), st],
        out_shape=[jax.ShapeDtypeStruct((n, L, e), F32), jax.ShapeDtypeStruct(s0.shape, F32)],
        scratch_shapes=[pltpu.VMEM((hp, dv, dk), F32)],
        compiler_params=_params("parallel", "parallel", "arbitrary"),
        name="hgrn_scan",
    )(qfvz, qfvz, qfvz, lb.reshape(1, e), g_norm.reshape(1, e), s0)


def kernel(x_prompt, x_sample, cache_attn_k, cache_attn_v, state_ssm_re, state_ssm_im, state_hgrn, page_table, c_prompt, c_sample, ada_w, ada_b, ln_g, ln_b, ssm_w_in, ssm_a_re, ssm_a_im, ssm_b_re, ssm_b_im, ssm_c_re, ssm_c_im, ssm_d, ssm_log_dt, ssm_w_glu, ssm_b_glu, ssm_w_out, attn_w_in, attn_lam_q1, attn_lam_k1, attn_lam_q2, attn_lam_k2, attn_subln, attn_w_out, hgrn_w_in, hgrn_lb_logits, hgrn_g_norm, hgrn_w_out):
    depth = ada_w.shape[0]
    alpha = (2 * depth) ** 0.25
    past_len = page_table.shape[1] * cache_attn_k.shape[2]
    ssm_layers = [i for i in range(depth) if i % N_MIXERS == 0]
    attn_layers = [i for i in range(depth) if i % N_MIXERS == 1]
    hgrn_layers = [i for i in range(depth) if i % N_MIXERS == 2]

    n_p, n_s = x_prompt.shape[0], x_sample.shape[0]
    rows = n_p + n_s
    rows_pad = -(-rows // SUBLANES) * SUBLANES
    c_all = jnp.pad(jnp.concatenate([c_prompt, c_sample], axis=0), ((0, rows_pad - rows), (0, 0)))
    mod_all = ada_modulation(c_all, ada_w, ada_b)

    lb_cum = jnp.cumsum(jax.nn.softmax(hgrn_lb_logits.astype(F32), axis=0), axis=0)
    lb_all = lb_cum - lb_cum[0]

    s5_tabs = [s5_tables(ssm_a_re[j], ssm_a_im[j], ssm_b_re[j], ssm_b_im[j], ssm_c_re[j], ssm_c_im[j],
                         ssm_log_dt[j]) for j in range(len(ssm_layers))]
    ntile = ssm_a_re.shape[1] // S5_TILE_GROUPS
    hgrn_heads = state_hgrn.shape[2]

    def run_group(x, mod_rows, pos, sample):
        n, L, _ = x.shape
        if sample:
            blk_in = dict(bb=min(n, 128), tl=L)
            blk_out = dict(bb=min(n, 32), tl=L)
            blk_rope = dict(bb=min(n, 32), tl=L)
            s5_blk = dict(bs=min(n, 16), t=L)
            hgrn_blk = dict(tc=L, hp=hgrn_heads)
        else:
            blk_in = dict(bb=1, tl=min(L, 1024))
            blk_out = dict(bb=1, tl=min(L, 256))
            blk_rope = dict(bb=1, tl=min(L, 256))
            s5_blk = dict(bs=min(n, 4), t=min(L, 512))
            hgrn_blk = dict(tc=min(L, 128), hp=hgrn_heads // 2)
        rope_tabs = rope_tables(pos)
        ssm_re, ssm_im, kv_k, kv_v, hgrn_s = [], [], [], [], []
        for i in range(depth):
            mod = mod_all[i, mod_rows][:, None, :]
            if i % N_MIXERS == 0:
                j = ssm_layers.index(i)
                uz = mod_inproj(x, mod, ssm_w_in[j].astype(BF16), **blk_in)
                bw, cw, lam_bar = s5_tabs[j]
                if sample:
                    h0r = state_ssm_re[:, j].astype(F32).reshape(n, ntile, SUBLANES, LANES)
                    h0i = state_ssm_im[:, j].astype(F32).reshape(n, ntile, SUBLANES, LANES)
                else:
                    h0r = jnp.zeros((n, ntile, SUBLANES, LANES), F32)
                    h0i = h0r
                y, hr, hi = s5_scan(uz, bw, cw, lam_bar, h0r, h0i, **s5_blk)
                x = s5_out_ln(y, uz, ssm_d[j], ssm_w_glu[j].astype(BF16), ssm_b_glu[j], x, mod,
                              ssm_w_out[j].astype(BF16), ln_g[i], ln_b[i], alpha, **blk_out)
                ssm_re.append(hr.reshape(n, -1, SSM_STATE))
                ssm_im.append(hi.reshape(n, -1, SSM_STATE))
            elif i % N_MIXERS == 1:
                j = attn_layers.index(i)
                lam_init = 0.8 - 0.6 * math.exp(-0.3 * i)
                lam = (jnp.exp(jnp.sum(attn_lam_q1[j].astype(F32) * attn_lam_k1[j].astype(F32)))
                       - jnp.exp(jnp.sum(attn_lam_q2[j].astype(F32) * attn_lam_k2[j].astype(F32)))
                       + lam_init)
                qkvz = mod_inproj(x, mod, attn_w_in[j].astype(BF16), **blk_in)
                e = qkvz.shape[-1] // 4
                heads = e // DIFF_ROW
                q_bf, k_f, k_bf, v_f, v_op = rope_qkv(qkvz, rope_tabs, transpose_v=not sample, **blk_rope)
                if sample:
                    a = cached_diff_attn(q_bf, k_bf, v_op, cache_attn_k, cache_attn_v, j, page_table, lam,
                                         attn_subln[j], 1.0 - lam_init)
                else:
                    tq = min(L, 1024)
                    a = flash_diff_attn(q_bf, k_bf, v_op, lam, attn_subln[j], 1.0 - lam_init, tq=tq, tk=tq)
                x = out_ln(a, qkvz, 3, x, mod, attn_w_out[j].astype(BF16), ln_g[i], ln_b[i], alpha, **blk_out)
                kv_k.append(k_f.reshape(n, L, heads, DIFF_ROW))
                kv_v.append(v_f.reshape(n, L, heads, DIFF_ROW))
            else:
                j = hgrn_layers.index(i)
                qfvz = mod_inproj(x, mod, hgrn_w_in[j].astype(BF16), **blk_in)
                if sample:
                    s0 = state_hgrn[:, j].astype(F32)
                else:
                    s0 = jnp.zeros((n,) + state_hgrn.shape[2:], F32)
                o, s_fin = hgrn_scan(qfvz, lb_all[i], hgrn_g_norm[j].astype(F32), s0, **hgrn_blk)
                x = out_ln(o, qfvz, 3, x, mod, hgrn_w_out[j].astype(BF16), ln_g[i], ln_b[i], alpha, **blk_out)
                hgrn_s.append(s_fin)
        return (x, jnp.stack(kv_k, axis=2), jnp.stack(kv_v, axis=2),
                jnp.stack(ssm_re, axis=1), jnp.stack(ssm_im, axis=1), jnp.stack(hgrn_s, axis=1))

    pos_p = jnp.arange(x_prompt.shape[1], dtype=jnp.int32)
    pos_s = past_len + jnp.arange(x_sample.shape[1], dtype=jnp.int32)
    y_p, k_p, v_p, sr_p, si_p, hs_p = run_group(x_prompt, slice(0, n_p), pos_p, False)
    y_s, k_s, v_s, sr_s, si_s, hs_s = run_group(x_sample, slice(n_p, n_p + n_s), pos_s, True)
    return (y_p, y_s, k_p, v_p, sr_p, si_p, hs_p, k_s, v_s, sr_s, si_s, hs_s)
```

```python
import functools
import math

import jax
import jax.numpy as jnp
from jax import lax
from jax.experimental import pallas as pl
from jax.experimental.pallas import tpu as pltpu

F32 = jnp.float32
BF16 = jnp.bfloat16

LANES = 128
SUBLANES = 8
BF16_SUBLANES = 16
VMEM_LIMIT_BYTES = 56 * 1024 * 1024

N_MIXERS = 3
SSM_GROUP = 16
SSM_STATE = 64
DIFF_HEAD_DIM = 64
DIFF_ROW = 2 * DIFF_HEAD_DIM
ROPE_DIM = DIFF_HEAD_DIM // 4
ROPE_THETA = 500000.0
LN_EPS = 1e-5
RMS_EPS = 1e-6
LOG2E = 1.4426950408889634

S5_TILE_GROUPS = 16
S5_TILE_CH = S5_TILE_GROUPS * SSM_GROUP
S5_TILE_STATES = S5_TILE_GROUPS * SSM_STATE
S5_CHUNKS = S5_TILE_STATES // LANES
assert S5_CHUNKS == SUBLANES

VT_ROWS = DIFF_ROW + BF16_SUBLANES


def _params(*sem):
    return pltpu.CompilerParams(dimension_semantics=sem, vmem_limit_bytes=VMEM_LIMIT_BYTES)


def _silu(x):
    return x * jax.nn.sigmoid(x)


def _ada_kernel(c_ref, w_ref, b_ref, o_ref):
    s = _silu(c_ref[...]).astype(BF16)
    o_ref[...] = jnp.dot(s, w_ref[...].astype(BF16), preferred_element_type=F32) + b_ref[...]


def ada_modulation(c_all, ada_w, ada_b):
    depth, d, d3 = ada_w.shape
    rows = c_all.shape[0]
    tn = 1024
    return pl.pallas_call(
        _ada_kernel,
        grid=(depth, d3 // tn),
        in_specs=[
            pl.BlockSpec((rows, d), lambda i, j: (0, 0)),
            pl.BlockSpec((None, d, tn), lambda i, j: (i, 0, j)),
            pl.BlockSpec((None, 1, tn), lambda i, j: (i, 0, j)),
        ],
        out_specs=pl.BlockSpec((None, rows, tn), lambda i, j: (i, 0, j)),
        out_shape=jax.ShapeDtypeStruct((depth, rows, d3), F32),
        compiler_params=_params("parallel", "parallel"),
        name="ada_modulation",
    )(c_all, ada_w, ada_b.reshape(depth, 1, d3))


def _inproj_kernel(x_ref, shift_ref, scale_ref, w_ref, o_ref):
    h = x_ref[...] * (1.0 + scale_ref[...]) + shift_ref[...]
    bb, tl, d = h.shape
    o = jnp.dot(h.reshape(bb * tl, d).astype(BF16), w_ref[...], preferred_element_type=F32)
    o_ref[...] = o.reshape(bb, tl, o.shape[-1])


def mod_inproj(x, mod, w_bf16, bb, tl, tn=1024):
    n, L, d = x.shape
    nout = w_bf16.shape[1]
    return pl.pallas_call(
        _inproj_kernel,
        grid=(n // bb, L // tl, nout // tn),
        in_specs=[
            pl.BlockSpec((bb, tl, d), lambda b, l, j: (b, l, 0)),
            pl.BlockSpec((bb, 1, d), lambda b, l, j: (b, 0, 0)),
            pl.BlockSpec((bb, 1, d), lambda b, l, j: (b, 0, 1)),
            pl.BlockSpec((d, tn), lambda b, l, j: (0, j)),
        ],
        out_specs=pl.BlockSpec((bb, tl, tn), lambda b, l, j: (b, l, j)),
        out_shape=jax.ShapeDtypeStruct((n, L, nout), F32),
        compiler_params=_params("parallel", "parallel", "arbitrary"),
        name="mod_inproj",
    )(x, mod, mod, w_bf16)


def _out_ln_kernel(a_ref, z_ref, x_ref, gate_ref, w_ref, g_ref, b_ref, o_ref, *, alpha):
    act = a_ref[...] * _silu(z_ref[...])
    bb, tl, e = act.shape
    out = jnp.dot(act.reshape(bb * tl, e).astype(BF16), w_ref[...], preferred_element_type=F32)
    out = out.reshape(bb, tl, out.shape[-1])
    r = alpha * x_ref[...] + (1.0 + gate_ref[...]) * out
    mu = jnp.mean(r, axis=-1, keepdims=True)
    dlt = r - mu
    var = jnp.mean(dlt * dlt, axis=-1, keepdims=True)
    o_ref[...] = dlt * lax.rsqrt(var + LN_EPS) * g_ref[...] + b_ref[...]


def out_ln(a, zsrc, zblk, x, mod, w_bf16, ln_g, ln_b, alpha, bb, tl):
    n, L, d = x.shape
    e = a.shape[-1]
    return pl.pallas_call(
        functools.partial(_out_ln_kernel, alpha=alpha),
        grid=(n // bb, L // tl),
        in_specs=[
            pl.BlockSpec((bb, tl, e), lambda b, l: (b, l, 0)),
            pl.BlockSpec((bb, tl, e), lambda b, l: (b, l, zblk)),
            pl.BlockSpec((bb, tl, d), lambda b, l: (b, l, 0)),
            pl.BlockSpec((bb, 1, d), lambda b, l: (b, 0, 2)),
            pl.BlockSpec((e, d), lambda b, l: (0, 0)),
            pl.BlockSpec((1, d), lambda b, l: (0, 0)),
            pl.BlockSpec((1, d), lambda b, l: (0, 0)),
        ],
        out_specs=pl.BlockSpec((bb, tl, d), lambda b, l: (b, l, 0)),
        out_shape=jax.ShapeDtypeStruct((n, L, d), F32),
        compiler_params=_params("parallel", "parallel"),
        name="out_ln",
    )(a, zsrc, x, mod, w_bf16, ln_g.reshape(1, d), ln_b.reshape(1, d))


def _s5_scan_kernel(u_ref, bw_ref, cw_ref, lam_ref, h0r_ref, h0i_ref, y_ref, hr_ref, hi_ref,
                    xr_scr, xi_scr, car_scr, *, bs, t):
    c = pl.program_id(2)
    nst = S5_TILE_STATES
    u = u_ref[...].reshape(bs * t, S5_TILE_CH).astype(BF16)
    bu = jnp.dot(u, bw_ref[...], preferred_element_type=F32)
    for s in range(bs):
        rows = slice(s * t, (s + 1) * t)
        for a in range(S5_CHUNKS):
            xr_scr[s, pl.ds(a, t, stride=SUBLANES), :] = bu[rows, a * LANES:(a + 1) * LANES]
            xi_scr[s, pl.ds(a, t, stride=SUBLANES), :] = bu[rows, nst + a * LANES:nst + (a + 1) * LANES]

    @pl.when(c == 0)
    def _():
        car_scr[0] = h0r_ref[...]
        car_scr[1] = h0i_ref[...]

    lr = lam_ref[0]
    li = lam_ref[1]

    def body(p, carry):
        r0 = pl.multiple_of(p * SUBLANES, SUBLANES)
        out = []
        for s in range(bs):
            hr, hi = carry[2 * s], carry[2 * s + 1]
            nr = lr * hr - li * hi + xr_scr[s, pl.ds(r0, SUBLANES), :]
            ni = lr * hi + li * hr + xi_scr[s, pl.ds(r0, SUBLANES), :]
            xr_scr[s, pl.ds(r0, SUBLANES), :] = nr
            xi_scr[s, pl.ds(r0, SUBLANES), :] = ni
            out += [nr, ni]
        return tuple(out)

    init = []
    for s in range(bs):
        init += [car_scr[0, s], car_scr[1, s]]
    fin = lax.fori_loop(0, t, body, tuple(init), unroll=SUBLANES)
    for s in range(bs):
        car_scr[0, s] = fin[2 * s]
        car_scr[1, s] = fin[2 * s + 1]

    for s in range(bs):
        parts = [xr_scr[s, pl.ds(a, t, stride=SUBLANES), :] for a in range(S5_CHUNKS)]
        parts += [xi_scr[s, pl.ds(a, t, stride=SUBLANES), :] for a in range(S5_CHUNKS)]
        hcat = jnp.concatenate(parts, axis=1).astype(BF16)
        y_ref[s] = jnp.dot(hcat, cw_ref[...], preferred_element_type=F32)

    @pl.when(c == pl.num_programs(2) - 1)
    def _():
        hr_ref[...] = car_scr[0]
        hi_ref[...] = car_scr[1]


def s5_scan(uz, bw, cw, lam, h0r, h0i, bs, t):
    n, L, _ = uz.shape
    ntile = bw.shape[0]
    nst = S5_TILE_STATES
    st_spec = pl.BlockSpec((bs, None, SUBLANES, LANES), lambda b, j, c: (b, j, 0, 0))
    return pl.pallas_call(
        functools.partial(_s5_scan_kernel, bs=bs, t=t),
        grid=(n // bs, ntile, L // t),
        in_specs=[
            pl.BlockSpec((bs, t, S5_TILE_CH), lambda b, j, c: (b, c, j)),
            pl.BlockSpec((None, S5_TILE_CH, 2 * nst), lambda b, j, c: (j, 0, 0)),
            pl.BlockSpec((None, 2 * nst, S5_TILE_CH), lambda b, j, c: (j, 0, 0)),
            pl.BlockSpec((None, 2, SUBLANES, LANES), lambda b, j, c: (j, 0, 0, 0)),
            st_spec, st_spec,
        ],
        out_specs=[
            pl.BlockSpec((bs, t, S5_TILE_CH), lambda b, j, c: (b, c, j)),
            st_spec, st_spec,
        ],
        out_shape=[
            jax.ShapeDtypeStruct((n, L, ntile * S5_TILE_CH), F32),
            jax.ShapeDtypeStruct((n, ntile, SUBLANES, LANES), F32),
            jax.ShapeDtypeStruct((n, ntile, SUBLANES, LANES), F32),
        ],
        scratch_shapes=[
            pltpu.VMEM((bs, t * SUBLANES, LANES), F32),
            pltpu.VMEM((bs, t * SUBLANES, LANES), F32),
            pltpu.VMEM((2, bs, SUBLANES, LANES), F32),
        ],
        compiler_params=_params("parallel", "parallel", "arbitrary"),
        name="s5_scan",
    )(uz, bw, cw, lam, h0r, h0i)


def _s5_out_kernel(y_ref, u_ref, z_ref, d_ref, wg_ref, bg_ref, x_ref, gate_ref, w_ref, g_ref, b_ref, o_ref, *,
                   alpha):
    y = jax.nn.gelu(y_ref[...] + d_ref[...] * u_ref[...])
    bb, tl, e = y.shape
    y2 = y.reshape(bb * tl, e)
    glu = jnp.dot(y2.astype(BF16), wg_ref[...], preferred_element_type=F32) + bg_ref[...]
    act = y2 * jax.nn.sigmoid(glu) * _silu(z_ref[...].reshape(bb * tl, e))
    out = jnp.dot(act.astype(BF16), w_ref[...], preferred_element_type=F32)
    out = out.reshape(bb, tl, out.shape[-1])
    r = alpha * x_ref[...] + (1.0 + gate_ref[...]) * out
    mu = jnp.mean(r, axis=-1, keepdims=True)
    dlt = r - mu
    var = jnp.mean(dlt * dlt, axis=-1, keepdims=True)
    o_ref[...] = dlt * lax.rsqrt(var + LN_EPS) * g_ref[...] + b_ref[...]


def s5_out_ln(y, uz, d_skip, w_glu_bf16, b_glu, x, mod, w_bf16, ln_g, ln_b, alpha, bb, tl):
    n, L, d = x.shape
    e = y.shape[-1]
    return pl.pallas_call(
        functools.partial(_s5_out_kernel, alpha=alpha),
        grid=(n // bb, L // tl),
        in_specs=[
            pl.BlockSpec((bb, tl, e), lambda b, l: (b, l, 0)),
            pl.BlockSpec((bb, tl, e), lambda b, l: (b, l, 0)),
            pl.BlockSpec((bb, tl, e), lambda b, l: (b, l, 1)),
            pl.BlockSpec((1, e), lambda b, l: (0, 0)),
            pl.BlockSpec((e, e), lambda b, l: (0, 0)),
            pl.BlockSpec((1, e), lambda b, l: (0, 0)),
            pl.BlockSpec((bb, tl, d), lambda b, l: (b, l, 0)),
            pl.BlockSpec((bb, 1, d), lambda b, l: (b, 0, 2)),
            pl.BlockSpec((e, d), lambda b, l: (0, 0)),
            pl.BlockSpec((1, d), lambda b, l: (0, 0)),
            pl.BlockSpec((1, d), lambda b, l: (0, 0)),
        ],
        out_specs=pl.BlockSpec((bb, tl, d), lambda b, l: (b, l, 0)),
        out_shape=jax.ShapeDtypeStruct((n, L, d), F32),
        compiler_params=_params("parallel", "parallel"),
        name="s5_out_ln",
    )(y, uz, uz, d_skip.reshape(1, e), w_glu_bf16, b_glu.reshape(1, e), x, mod, w_bf16,
      ln_g.reshape(1, d), ln_b.reshape(1, d))


def s5_tables(a_re, a_im, b_re, b_im, c_re, c_im, log_dt):
    g, p = a_re.shape
    ntile = g // S5_TILE_GROUPS
    dt = jnp.exp(log_dt.astype(F32))[:, None]
    lr, li = a_re.astype(F32), a_im.astype(F32)
    mag = jnp.exp(lr * dt)
    lb_re, lb_im = mag * jnp.cos(li * dt), mag * jnp.sin(li * dt)
    nr, ni = lb_re - 1.0, lb_im
    den = lr * lr + li * li
    w_re = (nr * lr + ni * li) / den
    w_im = (ni * lr - nr * li) / den
    br, bi = b_re.astype(F32), b_im.astype(F32)
    bb_re = w_re[..., None] * br - w_im[..., None] * bi
    bb_im = w_re[..., None] * bi + w_im[..., None] * br
    eye = jnp.eye(S5_TILE_GROUPS, dtype=F32)

    def in_layout(bb):
        t4 = bb.reshape(ntile, S5_TILE_GROUPS, p, SSM_GROUP)
        return jnp.einsum('jgpc,gh->jgchp', t4, eye).reshape(ntile, S5_TILE_CH, S5_TILE_STATES)

    def out_layout(cc):
        t4 = cc.reshape(ntile, S5_TILE_GROUPS, SSM_GROUP, p)
        return jnp.einsum('jgcp,gh->jgphc', t4, eye).reshape(ntile, S5_TILE_STATES, S5_TILE_CH)

    bw = jnp.concatenate([in_layout(bb_re), in_layout(bb_im)], axis=2).astype(BF16)
    cw = jnp.concatenate([out_layout(c_re.astype(F32)), -out_layout(c_im.astype(F32))], axis=1).astype(BF16)
    lam = jnp.stack([lb_re.reshape(ntile, SUBLANES, LANES), lb_im.reshape(ntile, SUBLANES, LANES)], axis=1)
    return bw, cw, lam


def _rope_kernel(q_ref, k_ref, v_ref, cos_ref, sa_ref, sb_ref, qo_ref, kf_ref, kb_ref, vf_ref, vo_ref, *,
                 qscale, transpose_v):
    cos = cos_ref[...][None]
    sa = sa_ref[...][None]
    sb = sb_ref[...][None]
    bb, tl, e = q_ref.shape

    def rot(x):
        x2 = x.reshape(bb * tl, LANES)
        up = pltpu.roll(x2, LANES - SUBLANES, 1).reshape(bb, tl, LANES)
        dn = pltpu.roll(x2, SUBLANES, 1).reshape(bb, tl, LANES)
        return x * cos + up * sa + dn * sb

    for tile in range(e // LANES):
        sl = slice(tile * LANES, (tile + 1) * LANES)
        qo_ref[:, :, sl] = (rot(q_ref[:, :, sl]) * qscale).astype(BF16)
        kr = rot(k_ref[:, :, sl])
        kf_ref[:, :, sl] = kr
        kb_ref[:, :, sl] = kr.astype(BF16)
        if transpose_v:
            vo_ref[tile, :DIFF_ROW, :] = v_ref[0, :, sl].T.astype(BF16)
            vo_ref[tile, DIFF_ROW:, :] = jnp.ones((VT_ROWS - DIFF_ROW, tl), BF16)
    if not transpose_v:
        vo_ref[...] = v_ref[...].astype(BF16)
    vf_ref[...] = v_ref[...]


def rope_tables(pos):
    half = ROPE_DIM // 2
    assert half == SUBLANES
    inv = ROPE_THETA ** (-jnp.arange(half, dtype=F32) / half)
    ang = pos.astype(F32)[:, None] * inv[None, :]
    cos, sin = jnp.cos(ang), jnp.sin(ang)
    lane = jnp.arange(LANES) % DIFF_HEAD_DIM
    idx = lane % half
    first = (lane < half)[None, :]
    second = ((lane >= half) & (lane < ROPE_DIM))[None, :]
    cos_t = jnp.where(first | second, cos[:, idx], 1.0)
    sa_t = jnp.where(first, -sin[:, idx], 0.0)
    sb_t = jnp.where(second, sin[:, idx], 0.0)
    return cos_t, sa_t, sb_t


def rope_qkv(qkvz, tables, bb, tl, transpose_v):
    n, L, e4 = qkvz.shape
    e = e4 // 4
    heads = e // DIFF_ROW
    cos_t, sa_t, sb_t = tables
    tab = pl.BlockSpec((tl, LANES), lambda b, l: (l, 0))
    blk = lambda col: pl.BlockSpec((bb, tl, e), lambda b, l: (b, l, col))
    if transpose_v:
        assert bb == 1
        v_spec = pl.BlockSpec((None, heads, VT_ROWS, tl), lambda b, l: (b, 0, 0, l))
        v_shape = jax.ShapeDtypeStruct((n, heads, VT_ROWS, L), BF16)
    else:
        v_spec = blk(0)
        v_shape = jax.ShapeDtypeStruct((n, L, e), BF16)
    return pl.pallas_call(
        functools.partial(_rope_kernel, qscale=LOG2E * DIFF_HEAD_DIM ** -0.5, transpose_v=transpose_v),
        grid=(n // bb, L // tl),
        in_specs=[blk(0), blk(1), blk(2), tab, tab, tab],
        out_specs=[blk(0), blk(0), blk(0), blk(0), v_spec],
        out_shape=[
            jax.ShapeDtypeStruct((n, L, e), BF16),
            jax.ShapeDtypeStruct((n, L, e), F32),
            jax.ShapeDtypeStruct((n, L, e), BF16),
            jax.ShapeDtypeStruct((n, L, e), F32),
            v_shape,
        ],
        compiler_params=_params("parallel", "parallel"),
        name="rope_qkv",
    )(qkvz, qkvz, qkvz, cos_t, sa_t, sb_t)


def _flash_kernel(lam_ref, q_ref, k_ref, vt_ref, g_ref, o_ref, m_scr, acc_scr, *, tq, tk, post_scale):
    i = pl.program_id(2)
    q = q_ref[...]
    lane = lax.broadcasted_iota(jnp.int32, q.shape, 1)
    zero = jnp.zeros_like(q)
    qmaps = (jnp.where(lane < DIFF_HEAD_DIM, q, zero), jnp.where(lane >= DIFF_HEAD_DIM, q, zero))
    contract_last = (((1,), (1,)), ((), ()))

    m_scr[...] = jnp.full(m_scr.shape, -jnp.inf, F32)
    acc_scr[...] = jnp.zeros(acc_scr.shape, F32)

    def block(j, mask):
        k0 = pl.multiple_of(j * tk, tk)
        k = k_ref[pl.ds(k0, tk), :]
        vt = vt_ref[:, pl.ds(k0, tk)]
        for mp in range(2):
            s = lax.dot_general(k, qmaps[mp], contract_last, preferred_element_type=F32)
            if mask is not None:
                s = jnp.where(mask, s, -jnp.inf)
            m_prev = m_scr[mp]
            m_new = jnp.maximum(m_prev, jnp.max(s, axis=0, keepdims=True))
            alpha = jnp.exp2(m_prev - m_new)
            p = jnp.exp2(s - m_new).astype(BF16)
            acc_scr[mp] = alpha * acc_scr[mp] + jnp.dot(vt, p, preferred_element_type=F32)
            m_scr[mp] = m_new

    def body(j, carry):
        block(j, None)
        return carry

    sub = tq // tk
    lax.fori_loop(0, i * sub, body, 0)
    keys = lax.broadcasted_iota(jnp.int32, (tk, tq), 0)
    qrys = lax.broadcasted_iota(jnp.int32, (tk, tq), 1)
    for d in range(sub):
        block(i * sub + d, keys + d * tk <= qrys)

    lam = lam_ref[0, 0]
    acc0 = acc_scr[0]
    acc1 = acc_scr[1]
    a = (acc0[:DIFF_ROW] * (1.0 / acc0[DIFF_ROW:DIFF_ROW + 1])
         - lam * (acc1[:DIFF_ROW] * (1.0 / acc1[DIFF_ROW:DIFF_ROW + 1])))
    ms = jnp.mean(a * a, axis=0, keepdims=True)
    a = a * lax.rsqrt(ms + RMS_EPS)
    o_ref[...] = a.T * g_ref[...] * post_scale


def flash_diff_attn(q, k, vt, lam, subln, post_scale, tq, tk):
    n, L, e = q.shape
    heads = e // DIFF_ROW
    return pl.pallas_call(
        functools.partial(_flash_kernel, tq=tq, tk=tk, post_scale=post_scale),
        grid=(n, heads, L // tq),
        in_specs=[
            pl.BlockSpec(memory_space=pltpu.SMEM),
            pl.BlockSpec((None, tq, DIFF_ROW), lambda b, h, i: (b, i, h)),
            pl.BlockSpec((None, L, DIFF_ROW), lambda b, h, i: (b, 0, h)),
            pl.BlockSpec((None, None, VT_ROWS, L), lambda b, h, i: (b, h, 0, 0)),
            pl.BlockSpec((1, DIFF_ROW), lambda b, h, i: (0, 0)),
        ],
        out_specs=pl.BlockSpec((None, tq, DIFF_ROW), lambda b, h, i: (b, i, h)),
        out_shape=jax.ShapeDtypeStruct((n, L, e), F32),
        scratch_shapes=[
            pltpu.VMEM((2, 1, tq), F32),
            pltpu.VMEM((2, VT_ROWS, tq), F32),
        ],
        compiler_params=_params("parallel", "parallel", "arbitrary"),
        name="flash_diff_attn",
    )(lam.reshape(1, 1), q, k, vt, subln.reshape(1, DIFF_ROW))


PAIR_ROWS = 4 * SUBLANES
PAIR_COLS = 2 * DIFF_ROW
PAGES_PER_STEP = 8


def _cached_attn_kernel(pt_ref, lam_ref, qd_ref, kn_ref, vn_ref, *rest, npairs, heads, page, npg, post_scale):
    del pt_ref
    kc_refs = rest[:npg]
    vc_refs = rest[npg:2 * npg]
    g_ref, o_ref, m_scr, l_scr, acc_scr = rest[2 * npg:]
    p = pl.program_id(1)
    nt = SUBLANES
    contract_last = (((1,), (1,)), ((), ()))

    def scores(key_tile):
        parts = [lax.dot_general(qd_ref[hp], key_tile(hp), contract_last, preferred_element_type=F32)
                 for hp in range(npairs)]
        return jnp.concatenate(parts, axis=0)

    def pair_slice(hp):
        return slice(hp * PAIR_ROWS, (hp + 1) * PAIR_ROWS)

    @pl.when(p == 0)
    def _():
        s = scores(lambda hp: kn_ref[:, hp * PAIR_COLS:(hp + 1) * PAIR_COLS])
        qpos = lax.broadcasted_iota(jnp.int32, s.shape, 0) % nt
        kpos = lax.broadcasted_iota(jnp.int32, s.shape, 1)
        s = jnp.where(kpos <= qpos, s, -jnp.inf)
        m = jnp.max(s, axis=1, keepdims=True)
        pr = jnp.exp2(s - m)
        m_scr[...] = m
        l_scr[...] = jnp.sum(pr, axis=1, keepdims=True)
        prb = pr.astype(BF16)
        for hp in range(npairs):
            acc_scr[hp] = jnp.dot(prb[pair_slice(hp)], vn_ref[:, hp * PAIR_COLS:(hp + 1) * PAIR_COLS],
                                  preferred_element_type=F32)

    def head_rows(ref, h):
        return ref[pl.ds(h, page, stride=heads), :]

    def page_tile(refs, hp):
        tiles = [jnp.concatenate([head_rows(ref, 2 * hp), head_rows(ref, 2 * hp + 1)], axis=1) for ref in refs]
        return jnp.concatenate(tiles, axis=0).astype(BF16)

    s = scores(lambda hp: page_tile(kc_refs, hp))
    m_prev = m_scr[...]
    m_new = jnp.maximum(m_prev, jnp.max(s, axis=1, keepdims=True))
    alpha = jnp.exp2(m_prev - m_new)
    pr = jnp.exp2(s - m_new)
    l_scr[...] = alpha * l_scr[...] + jnp.sum(pr, axis=1, keepdims=True)
    m_scr[...] = m_new
    prb = pr.astype(BF16)
    for hp in range(npairs):
        acc_scr[hp] = alpha[pair_slice(hp)] * acc_scr[hp] + jnp.dot(
            prb[pair_slice(hp)], page_tile(vc_refs, hp), preferred_element_type=F32)

    @pl.when(p == pl.num_programs(1) - 1)
    def _():
        lam = lam_ref[0, 0]
        l_all = l_scr[...]
        for hp in range(npairs):
            acc = acc_scr[hp]
            for hh in range(2):
                r0 = hp * PAIR_ROWS + hh * 2 * nt
                a0 = acc[hh * 2 * nt:hh * 2 * nt + nt, hh * DIFF_ROW:(hh + 1) * DIFF_ROW]
                a1 = acc[hh * 2 * nt + nt:(hh + 1) * 2 * nt, hh * DIFF_ROW:(hh + 1) * DIFF_ROW]
                a = a0 / l_all[r0:r0 + nt] - lam * (a1 / l_all[r0 + nt:r0 + 2 * nt])
                ms = jnp.mean(a * a, axis=-1, keepdims=True)
                h = 2 * hp + hh
                o_ref[:, h * DIFF_ROW:(h + 1) * DIFF_ROW] = a * lax.rsqrt(ms + RMS_EPS) * g_ref[...] * post_scale


def cached_diff_attn(q_bf, k_bf, v_bf, cache_k, cache_v, layer_j, page_table, lam, subln, post_scale):
    n, t, e = q_bf.shape
    assert t == SUBLANES
    heads = e // DIFF_ROW
    npairs = heads // 2
    n_pages = page_table.shape[1]
    n_pool, n_layers, page = cache_k.shape[:3]
    npg = math.gcd(PAGES_PER_STEP, n_pages)
    q5 = q_bf.reshape(n, t, npairs, 2, 2, DIFF_HEAD_DIM)
    eye2 = jnp.eye(2, dtype=BF16)
    qd = jnp.einsum('ntphjd,hg,jk->nphjtgkd', q5, eye2, eye2).reshape(n, npairs, PAIR_ROWS, PAIR_COLS)
    pad = ((0, 0), (0, BF16_SUBLANES - t), (0, 0))
    kn = jnp.pad(k_bf, pad)
    vn = jnp.pad(v_bf, pad)
    ck = cache_k.reshape(n_pool, n_layers, page * heads, DIFF_ROW)
    cv = cache_v.reshape(n_pool, n_layers, page * heads, DIFF_ROW)

    def cache_spec(i):
        return pl.BlockSpec((None, None, page * heads, DIFF_ROW),
                            lambda b, p, pt: (pt[b, npg * p + i], layer_j, 0, 0))

    new_spec = pl.BlockSpec((None, BF16_SUBLANES, e), lambda b, p, pt: (b, 0, 0))
    grid_spec = pltpu.PrefetchScalarGridSpec(
        num_scalar_prefetch=1,
        grid=(n, n_pages // npg),
        in_specs=[
            pl.BlockSpec(memory_space=pltpu.SMEM),
            pl.BlockSpec((None, npairs, PAIR_ROWS, PAIR_COLS), lambda b, p, pt: (b, 0, 0, 0)),
            new_spec, new_spec,
            *[cache_spec(i) for i in range(npg)],
            *[cache_spec(i) for i in range(npg)],
            pl.BlockSpec((1, DIFF_ROW), lambda b, p, pt: (0, 0)),
        ],
        out_specs=pl.BlockSpec((None, t, e), lambda b, p, pt: (b, 0, 0)),
        scratch_shapes=[
            pltpu.VMEM((npairs * PAIR_ROWS, 1), F32),
            pltpu.VMEM((npairs * PAIR_ROWS, 1), F32),
            pltpu.VMEM((npairs, PAIR_ROWS, PAIR_COLS), F32),
        ],
    )
    return pl.pallas_call(
        functools.partial(_cached_attn_kernel, npairs=npairs, heads=heads, page=page, npg=npg,
                          post_scale=post_scale),
        grid_spec=grid_spec,
        out_shape=jax.ShapeDtypeStruct((n, t, e), F32),
        compiler_params=_params("parallel", "arbitrary"),
        name="cached_diff_attn",
    )(page_table, lam.reshape(1, 1), qd, kn, vn, *([ck] * npg), *([cv] * npg), subln.reshape(1, DIFF_ROW))


def _hgrn_head(q, fz, v, lb, g, st, tc):
    dk = q.shape[1]
    f = lb + (1.0 - lb) * jax.nn.sigmoid(fz)
    logf = jnp.log(f)
    kk = (1.0 - lb) * jax.nn.sigmoid(-fz)
    row = lax.broadcasted_iota(jnp.int32, (tc, dk), 0)
    b = logf
    k = 1
    while k < tc:
        b = b + jnp.where(row >= k, pltpu.roll(b, k, 0), 0.0)
        k *= 2
    contract_last = (((1,), (1,)), ((), ()))
    contract_first = (((0,), (0,)), ((), ()))
    o = lax.dot_general((q * jnp.exp(b)).astype(BF16), st.astype(BF16), contract_last,
                        preferred_element_type=F32)
    rin = row % SUBLANES

    def shift(x, d):
        return pltpu.roll(x.reshape(tc // SUBLANES, SUBLANES, dk), d, 1).reshape(tc, dk)

    dec = None
    for d in range(SUBLANES):
        if d == 0:
            w = q * kk
            vr = v
        else:
            dec = f if d == 1 else dec * shift(f, d - 1)
            w = q * shift(kk, d) * dec
            vr = shift(v, d)
        cd = jnp.sum(jnp.where(rin >= d, w, 0.0), axis=1, keepdims=True)
        o = o + cd * vr
    if tc > SUBLANES:
        trow = lax.broadcasted_iota(jnp.int32, (tc, tc), 0)
        tcol = lax.broadcasted_iota(jnp.int32, (tc, tc), 1)
        att = jnp.zeros((tc, tc), F32)
        blk = 2 * SUBLANES
        while blk <= tc:
            half = blk // 2
            b3 = b.reshape(tc // blk, blk, dk)
            anchor = jnp.broadcast_to(b3[:, half - 1:half, :], b3.shape).reshape(tc, dk)
            pos = row % blk
            qt = jnp.where(pos >= half, q * jnp.exp(b - anchor), 0.0)
            kt = jnp.where(pos < half, kk * jnp.exp(anchor - b), 0.0)
            a = lax.dot_general(qt.astype(BF16), kt.astype(BF16), contract_last, preferred_element_type=F32)
            att = att + jnp.where((trow // blk) == (tcol // blk), a, 0.0)
            blk *= 2
        o = o + jnp.dot(att.astype(BF16), v.astype(BF16), preferred_element_type=F32)
    ms = jnp.mean(o * o, axis=-1, keepdims=True)
    o = o * lax.rsqrt(ms + RMS_EPS) * g
    b_end = b[tc - 1:, :]
    khat = kk * jnp.exp(b_end - b)
    st_new = st * jnp.exp(b_end) + lax.dot_general(v.astype(BF16), khat.astype(BF16), contract_first,
                                                   preferred_element_type=F32)
    return o, st_new


def _hgrn_kernel(q_ref, fz_ref, v_ref, lb_ref, g_ref, s0_ref, o_ref, sout_ref, st_scr, *, tc, hp):
    c = pl.program_id(2)

    @pl.when(c == 0)
    def _():
        for h in range(hp):
            st_scr[h] = s0_ref[h].T

    for h in range(hp):
        hs = slice(h * LANES, (h + 1) * LANES)
        o, st_new = _hgrn_head(q_ref[:, hs], fz_ref[:, hs], v_ref[:, hs], lb_ref[:, hs], g_ref[:, hs],
                               st_scr[h], tc)
        o_ref[:, hs] = o
        st_scr[h] = st_new

    @pl.when(c == pl.num_programs(2) - 1)
    def _():
        for h in range(hp):
            sout_ref[h] = st_scr[h].T


def hgrn_scan(qfvz, lb, g_norm, s0, tc, hp):
    n, L, e4 = qfvz.shape
    e = e4 // 4
    heads = s0.shape[1]
    dk, dv = s0.shape[2], s0.shape[3]
    assert dk == LANES and dv == LANES
    ng = heads // hp
    col = lambda base: pl.BlockSpec((None, tc, hp * dk), lambda b, h, c: (b, c, base * ng + h))
    vec = pl.BlockSpec((1, hp * dk), lambda b, h, c: (0, h))
    st = pl.BlockSpec((None, hp, dk, dv), lambda b, h, c: (b, h, 0, 0))
    return pl.pallas_call(
        functools.partial(_hgrn_kernel, tc=tc, hp=hp),
        grid=(n, ng, L // tc),
        in_specs=[col(0), col(1), col(2), vec, vec, st],
        out_specs=[pl.BlockSpec((None, tc, hp * dv), lambda b, h, c: (b, c, h)), st],
        out_shape=[jax.ShapeDtypeStruct((n, L, e), F32), jax.ShapeDtypeStruct(s0.shape, F32)],
        scratch_shapes=[pltpu.VMEM((hp, dv, dk), F32)],
        compiler_params=_params("parallel", "parallel", "arbitrary"),
        name="hgrn_scan",
    )(qfvz, qfvz, qfvz, lb.reshape(1, e), g_norm.reshape(1, e), s0)


def kernel(x_prompt, x_sample, cache_attn_k, cache_attn_v, state_ssm_re, state_ssm_im, state_hgrn, page_table, c_prompt, c_sample, ada_w, ada_b, ln_g, ln_b, ssm_w_in, ssm_a_re, ssm_a_im, ssm_b_re, ssm_b_im, ssm_c_re, ssm_c_im, ssm_d, ssm_log_dt, ssm_w_glu, ssm_b_glu, ssm_w_out, attn_w_in, attn_lam_q1, attn_lam_k1, attn_lam_q2, attn_lam_k2, attn_subln, attn_w_out, hgrn_w_in, hgrn_lb_logits, hgrn_g_norm, hgrn_w_out):
    depth = ada_w.shape[0]
    alpha = (2 * depth) ** 0.25
    past_len = page_table.shape[1] * cache_attn_k.shape[2]
    ssm_layers = [i for i in range(depth) if i % N_MIXERS == 0]
    attn_layers = [i for i in range(depth) if i % N_MIXERS == 1]
    hgrn_layers = [i for i in range(depth) if i % N_MIXERS == 2]

    n_p, n_s = x_prompt.shape[0], x_sample.shape[0]
    rows = n_p + n_s
    rows_pad = -(-rows // SUBLANES) * SUBLANES
    c_all = jnp.pad(jnp.concatenate([c_prompt, c_sample], axis=0), ((0, rows_pad - rows), (0, 0)))
    mod_all = ada_modulation(c_all, ada_w, ada_b)

    lb_cum = jnp.cumsum(jax.nn.softmax(hgrn_lb_logits.astype(F32), axis=0), axis=0)
    lb_all = lb_cum - lb_cum[0]

    s5_tabs = [s5_tables(ssm_a_re[j], ssm_a_im[j], ssm_b_re[j], ssm_b_im[j], ssm_c_re[j], ssm_c_im[j],
                         ssm_log_dt[j]) for j in range(len(ssm_layers))]
    ntile = ssm_a_re.shape[1] // S5_TILE_GROUPS
    hgrn_heads = state_hgrn.shape[2]

    def run_group(x, mod_rows, pos, sample):
        n, L, _ = x.shape
        if sample:
            blk_in = dict(bb=min(n, 128), tl=L)
            blk_out = dict(bb=min(n, 32), tl=L)
            blk_rope = dict(bb=min(n, 32), tl=L)
            s5_blk = dict(bs=min(n, 16), t=L)
            hgrn_blk = dict(tc=L, hp=hgrn_heads)
        else:
            blk_in = dict(bb=1, tl=min(L, 1024))
            blk_out = dict(bb=1, tl=min(L, 256))
            blk_rope = dict(bb=1, tl=min(L, 256))
            s5_blk = dict(bs=min(n, 4), t=min(L, 512))
            hgrn_blk = dict(tc=min(L, 128), hp=hgrn_heads)
        rope_tabs = rope_tables(pos)
        ssm_re, ssm_im, kv_k, kv_v, hgrn_s = [], [], [], [], []
        for i in range(depth):
            mod = mod_all[i, mod_rows][:, None, :]
            if i % N_MIXERS == 0:
                j = ssm_layers.index(i)
                uz = mod_inproj(x, mod, ssm_w_in[j].astype(BF16), **blk_in)
                bw, cw, lam_bar = s5_tabs[j]
                if sample:
                    h0r = state_ssm_re[:, j].astype(F32).reshape(n, ntile, SUBLANES, LANES)
                    h0i = state_ssm_im[:, j].astype(F32).reshape(n, ntile, SUBLANES, LANES)
                else:
                    h0r = jnp.zeros((n, ntile, SUBLANES, LANES), F32)
                    h0i = h0r
                y, hr, hi = s5_scan(uz, bw, cw, lam_bar, h0r, h0i, **s5_blk)
                x = s5_out_ln(y, uz, ssm_d[j], ssm_w_glu[j].astype(BF16), ssm_b_glu[j], x, mod,
                              ssm_w_out[j].astype(BF16), ln_g[i], ln_b[i], alpha, **blk_out)
                ssm_re.append(hr.reshape(n, -1, SSM_STATE))
                ssm_im.append(hi.reshape(n, -1, SSM_STATE))
            elif i % N_MIXERS == 1:
                j = attn_layers.index(i)
                lam_init = 0.8 - 0.6 * math.exp(-0.3 * i)
                lam = (jnp.exp(jnp.sum(attn_lam_q1[j].astype(F32) * attn_lam_k1[j].astype(F32)))
                       - jnp.exp(jnp.sum(attn_lam_q2[j].astype(F32) * attn_lam_k2[j].astype(F32)))
                       + lam_init)
                qkvz = mod_inproj(x, mod, attn_w_in[j].astype(BF16), **blk_in)
                e = qkvz.shape[-1] // 4
                heads = e // DIFF_ROW
                q_bf, k_f, k_bf, v_f, v_op = rope_qkv(qkvz, rope_tabs, transpose_v=not sample, **blk_rope)
                if sample:
                    a = cached_diff_attn(q_bf, k_bf, v_op, cache_attn_k, cache_attn_v, j, page_table, lam,
                                         attn_subln[j], 1.0 - lam_init)
                else:
                    tq = min(L, 1024)
                    a = flash_diff_attn(q_bf, k_bf, v_op, lam, attn_subln[j], 1.0 - lam_init, tq=tq, tk=tq)
                x = out_ln(a, qkvz, 3, x, mod, attn_w_out[j].astype(BF16), ln_g[i], ln_b[i], alpha, **blk_out)
                kv_k.append(k_f.reshape(n, L, heads, DIFF_ROW))
                kv_v.append(v_f.reshape(n, L, heads, DIFF_ROW))
            else:
                j = hgrn_layers.index(i)
                qfvz = mod_inproj(x, mod, hgrn_w_in[j].astype(BF16), **blk_in)
                if sample:
                    s0 = state_hgrn[:, j].astype(F32)
                else:
                    s0 = jnp.zeros((n,) + state_hgrn.shape[2:], F32)
                o, s_fin = hgrn_scan(qfvz, lb_all[i], hgrn_g_norm[j].astype(F32), s0, **hgrn_blk)
                x = out_ln(o, qfvz, 3, x, mod, hgrn_w_out[j].astype(BF16), ln_g[i], ln_b[i], alpha, **blk_out)
                hgrn_s.append(s_fin)
        return (x, jnp.stack(kv_k, axis=2), jnp.stack(kv_v, axis=2),
                jnp.stack(ssm_re, axis=1), jnp.stack(ssm_im, axis=1), jnp.stack(hgrn_s, axis=1))

    pos_p = jnp.arange(x_prompt.shape[1], dtype=jnp.int32)
    pos_s = past_len + jnp.arange(x_sample.shape[1], dtype=jnp.int32)
    y_p, k_p, v_p, sr_p, si_p, hs_p = run_group(x_prompt, slice(0, n_p), pos_p, False)
    y_s, k_s, v_s, sr_s, si_s, hs_s = run_group(x_sample, slice(n_p, n_p + n_s), pos_s, True)
    return (y_p, y_s, k_p, v_p, sr_p, si_p, hs_p, k_s, v_s, sr_s, si_s, hs_s)
```
